```python
import math
import jax, jax.numpy as jnp
from jax import lax
import numpy as np

D_MODEL = 2048
BATCH = 2
SEQ = 16384
DEPTH = 2

HEAD_DIM = 128
SGU_GROUPS = 8
SGU_CHUNK = 128
SGU_WIDTH = SGU_GROUPS * HEAD_DIM
NSA_HEADS = 8
NSA_KV_GROUPS = 2
NSA_REP = NSA_HEADS // NSA_KV_GROUPS
NSA_WIDTH = NSA_HEADS * HEAD_DIM
KV_WIDTH = NSA_KV_GROUPS * HEAD_DIM
CMP_LEN = 32
CMP_STRIDE = 16
SLC_LEN = 64
SLC_TOP_N = 16
WINDOW = 512
Q_BLOCK = 128
N_BRANCH = 3
FORCE_SCORE = 1e9
EVEN_IN_WIDTH = 2 * SGU_WIDTH + NSA_WIDTH + 6 * KV_WIDTH + N_BRANCH * NSA_HEADS
EVEN_MIX_WIDTH = SGU_WIDTH + NSA_WIDTH
CONV_WIDTH = D_MODEL
CONV_K = 3
D_FF = -(-8 * D_MODEL // (3 * 256)) * 256
N_EVEN = (DEPTH + 1) // 2
N_ODD = DEPTH // 2
EPS = 1e-6
NEG = -1e30

kernel_name = 'hybrid_sgu_nsa_shortconv_trunk'


def rmsnorm(x, g):
    xf = x.astype(jnp.float32)
    y = xf * lax.rsqrt(jnp.mean(xf * xf, axis=-1, keepdims=True) + EPS)
    return (y * g.astype(jnp.float32)).astype(x.dtype)


def masked_softmax(s, mask):
    s = jnp.where(mask, s.astype(jnp.float32), NEG)
    m = jnp.max(s, axis=-1, keepdims=True)
    p = jnp.exp(s - m) * mask
    return p / jnp.maximum(jnp.sum(p, axis=-1, keepdims=True), 1e-30)


def alibi_slopes(n):
    return 2.0 ** (-8.0 * jnp.arange(1, n + 1, dtype=jnp.float32) / n)


def swiglu(h, w_gate, w_up, w_down):
    return (jax.nn.silu(h @ w_gate) * (h @ w_up)) @ w_down


def compress_blocks(kv, pe, w1, w2):
    b, s, g, d = kv.shape
    r = CMP_LEN // CMP_STRIDE
    n_cmp = s // CMP_STRIDE - (r - 1)
    chunks = kv.reshape(b, s // CMP_STRIDE, CMP_STRIDE, g, d)
    blocks = jnp.concatenate([chunks[:, j:j + n_cmp] for j in range(r)], axis=2)
    blocks = blocks + pe[None, None, :, None, :]
    hid = jax.nn.gelu(jnp.einsum('bnlgd,lde->bnge', blocks, w1))
    return jnp.einsum('bnge,ef->bngf', hid, w2)


def selection_importance(p_cmp, n_slc):
    r1 = SLC_LEN // CMP_STRIDE
    r2 = CMP_LEN // CMP_STRIDE
    n_cmp = p_cmp.shape[-1]
    pad = [(0, 0)] * (p_cmp.ndim - 1) + [(r2 - 1, r1 * n_slc - n_cmp)]
    pp = jnp.pad(p_cmp, pad)
    terms = []
    for m in range(r1):
        for n in range(r2):
            st = m - n + r2 - 1
            terms.append(pp[..., st: st + r1 * (n_slc - 1) + 1: r1])
    return jnp.sum(jnp.stack(terms, axis=0), axis=0)


def nsa_attention(q, k_cmp, v_cmp, k_slc, v_slc, k_win, v_win, gates):
    b, s, g, r, d = q.shape
    n_cmp = k_cmp.shape[1]
    n_slc = s // SLC_LEN
    top_n = min(SLC_TOP_N, n_slc)
    scale = d ** -0.5
    slopes = alibi_slopes(g * r).reshape(g, r)
    cmp_idx = jnp.arange(n_cmp)
    cmp_end = cmp_idx * CMP_STRIDE + CMP_LEN - 1
    cmp_mid = (cmp_idx * CMP_STRIDE).astype(jnp.float32) + (CMP_LEN - 1) / 2
    ks_blocks = k_slc.reshape(b, n_slc, SLC_LEN, g, d).transpose(0, 3, 1, 2, 4)
    vs_blocks = v_slc.reshape(b, n_slc, SLC_LEN, g, d).transpose(0, 3, 1, 2, 4)
    kw_pad = jnp.pad(k_win, ((0, 0), (WINDOW, 0), (0, 0), (0, 0)))
    vw_pad = jnp.pad(v_win, ((0, 0), (WINDOW, 0), (0, 0), (0, 0)))
    b_ix = jnp.arange(b)[:, None, None, None]
    g_ix = jnp.arange(g)[None, :, None, None]
    blk_ids = jnp.arange(n_slc)

    def one_block(qb):
        q0 = qb * Q_BLOCK
        t = q0 + jnp.arange(Q_BLOCK)
        tf = t.astype(jnp.float32)
        qblk = lax.dynamic_slice_in_dim(q, q0, Q_BLOCK, axis=1)
        gblk = lax.dynamic_slice_in_dim(gates, q0, Q_BLOCK, axis=1)
        s_c = jnp.einsum('bqgrd,bngd->bgrqn', qblk, k_cmp).astype(jnp.float32) * scale
        s_c = s_c - slopes[None, :, :, None, None] * (tf[:, None] - cmp_mid[None, :])
        p_c = masked_softmax(s_c, cmp_end[None, :] <= t[:, None])
        o_c = jnp.einsum('bgrqn,bngd->bqgrd', p_c.astype(v_cmp.dtype), v_cmp)
        imp = selection_importance(jnp.sum(p_c, axis=2), n_slc)
        cur = t // SLC_LEN
        causal = blk_ids[None, :] <= cur[:, None]
        forced = (blk_ids[None, :] == 0) | (blk_ids[None, :] == cur[:, None]) | (blk_ids[None, :] == cur[:, None] - 1)
        score = jnp.where(forced, FORCE_SCORE, jnp.where(causal, imp, -1.0))
        _, idx = lax.top_k(score, top_n)
        kg = ks_blocks[b_ix, g_ix, idx]
        vg = vs_blocks[b_ix, g_ix, idx]
        key_pos = idx[..., None] * SLC_LEN + jnp.arange(SLC_LEN)
        mask_s = key_pos <= t[None, None, :, None, None]
        dist = tf[None, None, :, None, None] - key_pos.astype(jnp.float32)
        s_s = jnp.einsum('bqgrd,bgqnld->bgrqnl', qblk, kg).astype(jnp.float32) * scale
        s_s = s_s - slopes[None, :, :, None, None, None] * dist[:, :, None]
        p_s = masked_softmax(s_s.reshape(b, g, r, Q_BLOCK, -1), mask_s.reshape(b, g, 1, Q_BLOCK, -1))
        o_s = jnp.einsum('bgrqnl,bgqnld->bqgrd', p_s.reshape(s_s.shape).astype(vg.dtype), vg)
        kw = lax.dynamic_slice_in_dim(kw_pad, q0, Q_BLOCK + WINDOW, axis=1)
        vw = lax.dynamic_slice_in_dim(vw_pad, q0, Q_BLOCK + WINDOW, axis=1)
        s_pos = q0 - WINDOW + jnp.arange(Q_BLOCK + WINDOW)
        rel = t[:, None] - s_pos[None, :]
        mask_w = (rel >= 0) & (rel < WINDOW) & (s_pos[None, :] >= 0)
        s_w = jnp.einsum('bqgrd,bkgd->bgrqk', qblk, kw).astype(jnp.float32) * scale
        s_w = s_w - slopes[None, :, :, None, None] * rel.astype(jnp.float32)
        p_w = masked_softmax(s_w, mask_w)
        o_w = jnp.einsum('bgrqk,bkgd->bqgrd', p_w.astype(vw.dtype), vw)
        return gblk[..., 0:1] * o_c + gblk[..., 1:2] * o_s + gblk[..., 2:3] * o_w

    out = lax.map(one_block, jnp.arange(s // Q_BLOCK))
    return out.transpose(1, 0, 2, 3, 4, 5).reshape(b, s, g * r * d)


def sgu_nsa_mixer(h, w_in, w_out, sgu_w, sgu_b, sgu_g,
                  cmp_pe_k, cmp_w1_k, cmp_w2_k, cmp_pe_v, cmp_w1_v, cmp_w2_v):
    b, s, _ = h.shape
    proj = h @ w_in
    sizes = [SGU_WIDTH, SGU_WIDTH, NSA_WIDTH] + [KV_WIDTH] * 6 + [N_BRANCH * NSA_HEADS]
    splits = np.cumsum(sizes)[:-1].tolist()
    u, v, q, kc, vc, ks, vs, kw, vw, gt = jnp.split(proj, splits, axis=-1)
    u = jax.nn.gelu(u)
    v = rmsnorm(jax.nn.gelu(v).reshape(b, s, SGU_GROUPS, HEAD_DIM), sgu_g)
    w_causal = sgu_w * jnp.tril(jnp.ones((SGU_CHUNK, SGU_CHUNK), sgu_w.dtype))
    v_ch = v.reshape(b, s // SGU_CHUNK, SGU_CHUNK, SGU_GROUPS, HEAD_DIM)
    s_gate = jnp.einsum('gtp,bnpgc->bntgc', w_causal, v_ch) + sgu_b.T[None, None, :, :, None]
    y_a = u * s_gate.reshape(b, s, SGU_WIDTH)
    kv_shape = lambda z: z.reshape(b, s, NSA_KV_GROUPS, HEAD_DIM)
    k_cmp = compress_blocks(kv_shape(kc), cmp_pe_k, cmp_w1_k, cmp_w2_k)
    v_cmp = compress_blocks(kv_shape(vc), cmp_pe_v, cmp_w1_v, cmp_w2_v)
    gates = jax.nn.sigmoid(gt.reshape(b, s, NSA_KV_GROUPS, NSA_REP, N_BRANCH))
    y_b = nsa_attention(q.reshape(b, s, NSA_KV_GROUPS, NSA_REP, HEAD_DIM), k_cmp, v_cmp,
                        kv_shape(ks), kv_shape(vs), kv_shape(kw), kv_shape(vw), gates)
    return jnp.concatenate([y_a, y_b], axis=-1) @ w_out


def short_conv_mixer(h, w_in, conv_w, w_out):
    bg, cg, z = jnp.split(h @ w_in, 3, axis=-1)
    z = cg * z
    zc = lax.conv_general_dilated(z, conv_w[:, None, :], window_strides=(1,),
                                  padding=[(CONV_K - 1, 0)],
                                  dimension_numbers=('NWC', 'WIO', 'NWC'),
                                  feature_group_count=CONV_WIDTH)
    return (bg * zc) @ w_out


def setup_inputs(seed: int = 0) -> dict:
    key = jax.random.key(seed)
    ks = jax.random.split(key, 24)
    nrm = lambda k, shape, fan_in: jax.random.normal(k, shape, jnp.float32) * fan_in ** -0.5
    gain = lambda k, shape: 1.0 + 0.1 * jax.random.normal(k, shape, jnp.float32)
    return {
        'x': jax.random.normal(ks[0], (BATCH, SEQ, D_MODEL), jnp.float32),
        'norm_mix': gain(ks[1], (DEPTH, D_MODEL)),
        'norm_ffn': gain(ks[2], (DEPTH, D_MODEL)),
        'norm_f': gain(ks[3], (D_MODEL,)),
        'w_in_ab': nrm(ks[4], (N_EVEN, D_MODEL, EVEN_IN_WIDTH), D_MODEL),
        'w_out_ab': nrm(ks[5], (N_EVEN, EVEN_MIX_WIDTH, D_MODEL), EVEN_MIX_WIDTH),
        'sgu_w': nrm(ks[6], (N_EVEN, SGU_GROUPS, SGU_CHUNK, SGU_CHUNK), SGU_CHUNK),
        'sgu_b': gain(ks[7], (N_EVEN, SGU_GROUPS, SGU_CHUNK)),
        'sgu_g': gain(ks[8], (N_EVEN, SGU_GROUPS, HEAD_DIM)),
        'cmp_pe_k': 0.1 * jax.random.normal(ks[9], (N_EVEN, CMP_LEN, HEAD_DIM), jnp.float32),
        'cmp_w1_k': nrm(ks[10], (N_EVEN, CMP_LEN, HEAD_DIM, HEAD_DIM), CMP_LEN * HEAD_DIM),
        'cmp_w2_k': nrm(ks[11], (N_EVEN, HEAD_DIM, HEAD_DIM), HEAD_DIM),
        'cmp_pe_v': 0.1 * jax.random.normal(ks[12], (N_EVEN, CMP_LEN, HEAD_DIM), jnp.float32),
        'cmp_w1_v': nrm(ks[13], (N_EVEN, CMP_LEN, HEAD_DIM, HEAD_DIM), CMP_LEN * HEAD_DIM),
        'cmp_w2_v': nrm(ks[14], (N_EVEN, HEAD_DIM, HEAD_DIM), HEAD_DIM),
        'w_in_c': nrm(ks[15], (N_ODD, D_MODEL, 3 * CONV_WIDTH), D_MODEL),
        'conv_w': nrm(ks[16], (N_ODD, CONV_K, CONV_WIDTH), CONV_K),
        'w_out_c': nrm(ks[17], (N_ODD, CONV_WIDTH, D_MODEL), CONV_WIDTH),
        'w_gate': nrm(ks[18], (DEPTH, D_MODEL, D_FF), D_MODEL),
        'w_up': nrm(ks[19], (DEPTH, D_MODEL, D_FF), D_MODEL),
        'w_down': nrm(ks[20], (DEPTH, D_FF, D_MODEL), D_FF),
    }


def reference(x, norm_mix, norm_ffn, norm_f, w_in_ab, w_out_ab, sgu_w, sgu_b, sgu_g,
              cmp_pe_k, cmp_w1_k, cmp_w2_k, cmp_pe_v, cmp_w1_v, cmp_w2_v,
              w_in_c, conv_w, w_out_c, w_gate, w_up, w_down):
    h = x
    for i in range(DEPTH):
        j = i // 2
        hn = rmsnorm(h, norm_mix[i])
        if i % 2 == 0:
            mix = sgu_nsa_mixer(hn, w_in_ab[j], w_out_ab[j], sgu_w[j], sgu_b[j], sgu_g[j],
                                cmp_pe_k[j], cmp_w1_k[j], cmp_w2_k[j],
                                cmp_pe_v[j], cmp_w1_v[j], cmp_w2_v[j])
        else:
            mix = short_conv_mixer(hn, w_in_c[j], conv_w[j], w_out_c[j])
        h = h + mix
        h = h + swiglu(rmsnorm(h, norm_ffn[i]), w_gate[i], w_up[i], w_down[i])
    return rmsnorm(h, norm_f)
```

```python
import functools
import math

import jax
import jax.numpy as jnp
import numpy as np
from jax import lax
from jax.experimental import pallas as pl
from jax.experimental.pallas import tpu as pltpu

F32 = jnp.float32
BF16 = jnp.bfloat16

HEAD_DIM = 128
SGU_GROUPS = 8
SGU_CHUNK = 128
SGU_WIDTH = SGU_GROUPS * HEAD_DIM
NSA_HEADS = 8
NSA_KV_GROUPS = 2
NSA_REP = NSA_HEADS // NSA_KV_GROUPS
NSA_WIDTH = NSA_HEADS * HEAD_DIM
KV_WIDTH = NSA_KV_GROUPS * HEAD_DIM
GROUP_Q_WIDTH = NSA_REP * HEAD_DIM
CMP_LEN = 32
CMP_STRIDE = 16
SLC_LEN = 64
SLC_TOP_N = 16
WINDOW = 512
Q_BLOCK = 128
N_BRANCH = 3
FORCE_SCORE = 1e9
EPS = 1e-6
NEG = -1e30
REMOVED = -3e38
ATTN_SCALE = HEAD_DIM ** -0.5

COL_U = 0
COL_V = SGU_WIDTH
COL_Q = 2 * SGU_WIDTH
COL_KC = COL_Q + NSA_WIDTH
COL_VC = COL_KC + KV_WIDTH
COL_KS = COL_VC + KV_WIDTH
COL_VS = COL_KS + KV_WIDTH
COL_KW = COL_VS + KV_WIDTH
COL_VW = COL_KW + KV_WIDTH
COL_GT = COL_VW + KV_WIDTH

LANES = 128
VMEM_LIMIT = 56 * 1024 * 1024


def _params(sem, vmem=VMEM_LIMIT):
    return pltpu.CompilerParams(dimension_semantics=sem, vmem_limit_bytes=vmem)


def _rms_scale(x, gain_row):
    ms = jnp.mean(x * x, axis=-1, keepdims=True)
    return x * lax.rsqrt(ms + EPS) * gain_row


def _masked_softmax(s, mask):
    s = jnp.where(mask, s, NEG)
    m = jnp.max(s, axis=-1, keepdims=True)
    p = jnp.exp(s - m) * mask.astype(F32)
    denom = jnp.maximum(jnp.sum(p, axis=-1, keepdims=True), 1e-30)
    return p * (1.0 / denom)


def _dot_nt(a, b):
    return lax.dot_general(a, b, (((1,), (1,)), ((), ())), preferred_element_type=F32)


def _norm_matmul_kernel(x_ref, g_ref, w_ref, o_ref, xn_ref):
    @pl.when(pl.program_id(1) == 0)
    def _():
        xn_ref[...] = _rms_scale(x_ref[...], g_ref[...]).astype(BF16)

    o_ref[...] = jnp.dot(xn_ref[...], w_ref[...], preferred_element_type=F32).astype(o_ref.dtype)


def _norm_matmul(x, gain, w, out_dtype, tm, tn):
    m, d = x.shape
    n = w.shape[1]
    return pl.pallas_call(
        _norm_matmul_kernel,
        grid=(m // tm, n // tn),
        in_specs=[
            pl.BlockSpec((tm, d), lambda i, j: (i, 0)),
            pl.BlockSpec((1, d), lambda i, j: (0, 0)),
            pl.BlockSpec((d, tn), lambda i, j: (0, j)),
        ],
        out_specs=pl.BlockSpec((tm, tn), lambda i, j: (i, j)),
        out_shape=jax.ShapeDtypeStruct((m, n), out_dtype),
        scratch_shapes=[pltpu.VMEM((tm, d), BF16)],
        compiler_params=_params(("arbitrary", "arbitrary")),
        name="norm_matmul",
    )(x, gain.reshape(1, d), w)


def _sgu_kernel(u_ref, v_ref, w_ref, bt_ref, gain_ref, o_ref, *, chunks):
    row = lax.broadcasted_iota(jnp.int32, (SGU_CHUNK, SGU_CHUNK), 0)
    col = lax.broadcasted_iota(jnp.int32, (SGU_CHUNK, SGU_CHUNK), 1)
    tri = row >= col
    for g in range(SGU_GROUPS):
        w = jnp.where(tri, w_ref[g], 0.0).astype(BF16)
        bias = bt_ref[:, g:g + 1]
        gain = gain_ref[g:g + 1, :]
        cs = slice(g * HEAD_DIM, (g + 1) * HEAD_DIM)
        for c in range(chunks):
            rs = slice(c * SGU_CHUNK, (c + 1) * SGU_CHUNK)
            u = jax.nn.gelu(u_ref[rs, cs].astype(F32))
            v = _rms_scale(jax.nn.gelu(v_ref[rs, cs].astype(F32)), gain)
            sg = jnp.dot(w, v.astype(BF16), preferred_element_type=F32) + bias
            o_ref[rs, cs] = (u * sg).astype(o_ref.dtype)


def _sgu(proj, sgu_w, sgu_b, sgu_g, tr):
    m = proj.shape[0]
    chunks = tr // SGU_CHUNK
    return pl.pallas_call(
        functools.partial(_sgu_kernel, chunks=chunks),
        grid=(m // tr,),
        in_specs=[
            pl.BlockSpec((tr, SGU_WIDTH), lambda i: (i, COL_U // SGU_WIDTH)),
            pl.BlockSpec((tr, SGU_WIDTH), lambda i: (i, COL_V // SGU_WIDTH)),
            pl.BlockSpec((SGU_GROUPS, SGU_CHUNK, SGU_CHUNK), lambda i: (0, 0, 0)),
            pl.BlockSpec((SGU_CHUNK, SGU_GROUPS), lambda i: (0, 0)),
            pl.BlockSpec((SGU_GROUPS, HEAD_DIM), lambda i: (0, 0)),
        ],
        out_specs=pl.BlockSpec((tr, SGU_WIDTH), lambda i: (i, 0)),
        out_shape=jax.ShapeDtypeStruct((m, SGU_WIDTH), BF16),
        compiler_params=_params(("arbitrary",)),
        name="sgu_mixer",
    )(proj, proj, sgu_w, sgu_b.T, sgu_g)


def _compress_kernel(x_ref, pe_ref, w1_ref, w2_ref, o_ref, xf_ref, *, seq):
    nc = seq // CMP_STRIDE
    xf_ref[0:seq, :] = x_ref[0].astype(F32)
    xf_ref[seq:seq + CMP_STRIDE, :] = jnp.zeros((CMP_STRIDE, HEAD_DIM), F32)
    acc = jnp.zeros((nc, HEAD_DIM), F32)
    for l in range(CMP_LEN):
        rows = xf_ref[pl.ds(l, nc, stride=CMP_STRIDE), :] + pe_ref[0, l:l + 1, :]
        acc = acc + jnp.dot(rows.astype(BF16), w1_ref[0, l], preferred_element_type=F32)
    hid = jax.nn.gelu(acc)
    o_ref[0, 0] = jnp.dot(hid.astype(BF16), w2_ref[0], preferred_element_type=F32).astype(o_ref.dtype)


def _compress(proj3, pe, w1, w2):
    b, s, _ = proj3.shape
    nc = s // CMP_STRIDE
    g = NSA_KV_GROUPS
    col0 = COL_KC // HEAD_DIM
    return pl.pallas_call(
        functools.partial(_compress_kernel, seq=s),
        grid=(2, b, g),
        in_specs=[
            pl.BlockSpec((1, s, HEAD_DIM), lambda kv, bi, gi: (bi, 0, col0 + g * kv + gi)),
            pl.BlockSpec((1, CMP_LEN, HEAD_DIM), lambda kv, bi, gi: (kv, 0, 0)),
            pl.BlockSpec((1, CMP_LEN, HEAD_DIM, HEAD_DIM), lambda kv, bi, gi: (kv, 0, 0, 0)),
            pl.BlockSpec((1, HEAD_DIM, HEAD_DIM), lambda kv, bi, gi: (kv, 0, 0)),
        ],
        out_specs=pl.BlockSpec((1, 1, nc, HEAD_DIM), lambda kv, bi, gi: (kv, bi * g + gi, 0, 0)),
        out_shape=jax.ShapeDtypeStruct((2, b * g, nc, HEAD_DIM), BF16),
        scratch_shapes=[pltpu.VMEM((s + CMP_STRIDE, HEAD_DIM), F32)],
        compiler_params=_params(("arbitrary", "arbitrary", "arbitrary")),
        name="kv_compress",
    )(proj3, pe, w1, w2)


def _importance_matrix(nc, n_slc):
    r1 = SLC_LEN // CMP_STRIDE
    r2 = CMP_LEN // CMP_STRIDE
    mat = np.zeros((nc, n_slc), np.float32)
    for j in range(n_slc):
        for m_ in range(r1):
            for n_ in range(r2):
                n = r1 * j + m_ - n_
                if 0 <= n < nc:
                    mat[n, j] += 1.0
    return mat


def _cmp_select_kernel(slopes_ref, q_ref, kc_ref, vc_ref, m_ref, oc_ref, idx_ref, *, nc, n_slc, top_n):
    g = pl.program_id(1)
    q0 = pl.program_id(2) * Q_BLOCK
    t = q0 + lax.broadcasted_iota(jnp.int32, (Q_BLOCK, 1), 0)
    n_ix = lax.broadcasted_iota(jnp.int32, (1, nc), 1)
    valid = (n_ix * CMP_STRIDE + (CMP_LEN - 1)) <= t
    dist = t.astype(F32) - ((n_ix * CMP_STRIDE).astype(F32) + (CMP_LEN - 1) / 2)
    kc = kc_ref[0, 0]
    vc = vc_ref[0, 0]
    psum = jnp.zeros((Q_BLOCK, nc), F32)
    for h in range(NSA_REP):
        cs = slice(h * HEAD_DIM, (h + 1) * HEAD_DIM)
        slope = slopes_ref[g * NSA_REP + h]
        s = _dot_nt(q_ref[0, :, cs], kc) * ATTN_SCALE - slope * dist
        p = _masked_softmax(s, valid)
        psum = psum + p
        oc_ref[0, :, cs] = jnp.dot(p.astype(BF16), vc, preferred_element_type=F32).astype(oc_ref.dtype)
    hi = psum.astype(BF16)
    lo = (psum - hi.astype(F32)).astype(BF16)
    mat = m_ref[...]
    imp = jnp.dot(hi, mat, preferred_element_type=F32) + jnp.dot(lo, mat, preferred_element_type=F32)
    j_ix = lax.broadcasted_iota(jnp.int32, (1, n_slc), 1)
    cur = t // SLC_LEN
    causal = j_ix <= cur
    forced = (j_ix == 0) | (j_ix == cur) | (j_ix == cur - 1)
    score = jnp.where(forced, FORCE_SCORE, jnp.where(causal, imp, -1.0))
    j_f = j_ix.astype(F32)
    r_ix = lax.broadcasted_iota(jnp.int32, (1, top_n), 1)
    idx = jnp.zeros((Q_BLOCK, top_n), jnp.int32)
    for r in range(top_n):
        best = jnp.max(score, axis=-1, keepdims=True)
        pick = jnp.min(jnp.where(score == best, j_f, float(n_slc)), axis=-1, keepdims=True)
        score = jnp.where(j_f == pick, REMOVED, score)
        idx = jnp.where(r_ix == r, pick.astype(jnp.int32), idx)
    idx_ref[0, 0] = idx


def _cmp_select(proj3, cmp_kv, slopes, imp_mat, top_n):
    b, s, _ = proj3.shape
    g = NSA_KV_GROUPS
    nc = s // CMP_STRIDE
    n_slc = s // SLC_LEN
    grid_spec = pltpu.PrefetchScalarGridSpec(
        num_scalar_prefetch=1,
        grid=(b, g, s // Q_BLOCK),
        in_specs=[
            pl.BlockSpec((1, Q_BLOCK, GROUP_Q_WIDTH), lambda bi, gi, qi, sl: (bi, qi, COL_Q // GROUP_Q_WIDTH + gi)),
            pl.BlockSpec((1, 1, nc, HEAD_DIM), lambda bi, gi, qi, sl: (0, bi * g + gi, 0, 0)),
            pl.BlockSpec((1, 1, nc, HEAD_DIM), lambda bi, gi, qi, sl: (1, bi * g + gi, 0, 0)),
            pl.BlockSpec((nc, n_slc), lambda bi, gi, qi, sl: (0, 0)),
        ],
        out_specs=[
            pl.BlockSpec((1, Q_BLOCK, GROUP_Q_WIDTH), lambda bi, gi, qi, sl: (bi, qi, gi)),
            pl.BlockSpec((1, 1, Q_BLOCK, top_n), lambda bi, gi, qi, sl: (bi, gi, qi, 0)),
        ],
    )
    return pl.pallas_call(
        functools.partial(_cmp_select_kernel, nc=nc, n_slc=n_slc, top_n=top_n),
        grid_spec=grid_spec,
        out_shape=[
            jax.ShapeDtypeStruct((b, s, NSA_WIDTH), F32),
            jax.ShapeDtypeStruct((b, g, s, top_n), jnp.int32),
        ],
        compiler_params=_params(("arbitrary", "arbitrary", "arbitrary")),
        name="cmp_attn_select",
    )(slopes, proj3, cmp_kv, cmp_kv, imp_mat)


PAIR = 2
PAIR_ROWS = PAIR * NSA_REP


def _slc_kernel(slopes_ref, idx_ref, q_ref, k_ref, v_ref, o_ref, qs_ref, os_ref, *, top_n):
    g = pl.program_id(1)
    q0 = pl.program_id(2) * Q_BLOCK
    n_keys = top_n * SLC_LEN
    for h in range(NSA_REP):
        qs_ref[pl.ds(h, Q_BLOCK, stride=NSA_REP), :] = q_ref[0, :, h * HEAD_DIM:(h + 1) * HEAD_DIM].astype(F32)
    row = lax.broadcasted_iota(jnp.int32, (PAIR_ROWS, 1), 0)
    head = row % NSA_REP
    slope_rows = jnp.zeros((PAIR_ROWS, 1), F32)
    for h in range(NSA_REP):
        slope_rows = jnp.where(head == h, slopes_ref[g * NSA_REP + h], slope_rows)
    lane = lax.broadcasted_iota(jnp.int32, (1, LANES), 1)
    blocks_per_vreg = LANES // SLC_LEN

    def pair_body(p, carry):
        base = pl.multiple_of(p * PAIR_ROWS, PAIR_ROWS)
        lhs = qs_ref[pl.ds(base, PAIR_ROWS), :].astype(BF16)
        out = jnp.zeros((PAIR_ROWS, HEAD_DIM), F32)
        for u in range(PAIR):
            tloc = p * PAIR + u
            t = q0 + tloc
            ids = [idx_ref[0, 0, tloc, r] for r in range(top_n)]
            offs = [pl.multiple_of(j * SLC_LEN, SLC_LEN) for j in ids]
            k_sel = jnp.concatenate([k_ref[0, pl.ds(o, SLC_LEN), :] for o in offs], axis=0)
            v_sel = jnp.concatenate([v_ref[0, pl.ds(o, SLC_LEN), :] for o in offs], axis=0)
            pos = []
            for c in range(n_keys // LANES):
                kp = lane + ids[c * blocks_per_vreg] * SLC_LEN
                for e in range(1, blocks_per_vreg):
                    kp = jnp.where(lane >= e * SLC_LEN, lane + (ids[c * blocks_per_vreg + e] - e) * SLC_LEN, kp)
                pos.append(kp)
            key_pos = jnp.concatenate(pos, axis=1)
            mask = key_pos <= t
            dist = (t - key_pos).astype(F32)
            s = _dot_nt(lhs, k_sel) * ATTN_SCALE - slope_rows * dist
            prob = _masked_softmax(s, mask)
            o_u = jnp.dot(prob.astype(BF16), v_sel, preferred_element_type=F32)
            out = jnp.where(row // NSA_REP == u, o_u, out)
        os_ref[pl.ds(base, PAIR_ROWS), :] = out
        return carry

    lax.fori_loop(0, Q_BLOCK // PAIR, pair_body, 0)
    for h in range(NSA_REP):
        o_ref[0, :, h * HEAD_DIM:(h + 1) * HEAD_DIM] = os_ref[pl.ds(h, Q_BLOCK, stride=NSA_REP), :].astype(o_ref.dtype)


def _slc_attention(proj3, idx, slopes, top_n):
    b, s, _ = proj3.shape
    g = NSA_KV_GROUPS
    grid_spec = pltpu.PrefetchScalarGridSpec(
        num_scalar_prefetch=1,
        grid=(b, g, s // Q_BLOCK),
        in_specs=[
            pl.BlockSpec((1, 1, Q_BLOCK, top_n), lambda bi, gi, qi, sl: (bi, gi, qi, 0), memory_space=pltpu.SMEM),
            pl.BlockSpec((1, Q_BLOCK, GROUP_Q_WIDTH), lambda bi, gi, qi, sl: (bi, qi, COL_Q // GROUP_Q_WIDTH + gi)),
            pl.BlockSpec((1, s, HEAD_DIM), lambda bi, gi, qi, sl: (bi, 0, COL_KS // HEAD_DIM + gi)),
            pl.BlockSpec((1, s, HEAD_DIM), lambda bi, gi, qi, sl: (bi, 0, COL_VS // HEAD_DIM + gi)),
        ],
        out_specs=pl.BlockSpec((1, Q_BLOCK, GROUP_Q_WIDTH), lambda bi, gi, qi, sl: (bi, qi, gi)),
        scratch_shapes=[
            pltpu.VMEM((Q_BLOCK * NSA_REP, HEAD_DIM), F32),
            pltpu.VMEM((Q_BLOCK * NSA_REP, HEAD_DIM), F32),
        ],
    )
    return pl.pallas_call(
        functools.partial(_slc_kernel, top_n=top_n),
        grid_spec=grid_spec,
        out_shape=jax.ShapeDtypeStruct((b, s, NSA_WIDTH), F32),
        compiler_params=_params(("arbitrary", "arbitrary", "arbitrary")),
        name="slc_attn",
    )(slopes, idx, proj3, proj3, proj3)


def _win_kernel(slopes_ref, q_ref, k_ref, v_ref, oc_ref, os_ref, gt_ref, y_ref, *, span):
    g = pl.program_id(1)
    q0 = pl.program_id(2) * Q_BLOCK
    start = pl.multiple_of(jnp.maximum(q0 - WINDOW, 0), Q_BLOCK)
    kw = k_ref[0, pl.ds(start, span), :]
    vw = v_ref[0, pl.ds(start, span), :]
    t = q0 + lax.broadcasted_iota(jnp.int32, (Q_BLOCK, 1), 0)
    s_pos = start + lax.broadcasted_iota(jnp.int32, (1, span), 1)
    rel = t - s_pos
    mask = (rel >= 0) & (rel < WINDOW)
    rel_f = rel.astype(F32)
    gates = jax.nn.sigmoid(gt_ref[...])
    for h in range(NSA_REP):
        cs = slice(h * HEAD_DIM, (h + 1) * HEAD_DIM)
        slope = slopes_ref[g * NSA_REP + h]
        s = _dot_nt(q_ref[0, :, cs], kw) * ATTN_SCALE - slope * rel_f
        p = _masked_softmax(s, mask)
        ow = jnp.dot(p.astype(BF16), vw, preferred_element_type=F32)
        c = h * N_BRANCH
        y = gates[:, c:c + 1] * oc_ref[0, :, cs] + gates[:, c + 1:c + 2] * os_ref[0, :, cs] + gates[:, c + 2:c + 3] * ow
        y_ref[0, :, cs] = y.astype(y_ref.dtype)


def _win_combine(proj3, o_cmp, o_slc, gates_raw, slopes):
    b, s, _ = proj3.shape
    g = NSA_KV_GROUPS
    nq = s // Q_BLOCK
    span = min(WINDOW + Q_BLOCK, s)
    grid_spec = pltpu.PrefetchScalarGridSpec(
        num_scalar_prefetch=1,
        grid=(b, g, nq),
        in_specs=[
            pl.BlockSpec((1, Q_BLOCK, GROUP_Q_WIDTH), lambda bi, gi, qi, sl: (bi, qi, COL_Q // GROUP_Q_WIDTH + gi)),
            pl.BlockSpec((1, s, HEAD_DIM), lambda bi, gi, qi, sl: (bi, 0, COL_KW // HEAD_DIM + gi)),
            pl.BlockSpec((1, s, HEAD_DIM), lambda bi, gi, qi, sl: (bi, 0, COL_VW // HEAD_DIM + gi)),
            pl.BlockSpec((1, Q_BLOCK, GROUP_Q_WIDTH), lambda bi, gi, qi, sl: (bi, qi, gi)),
            pl.BlockSpec((1, Q_BLOCK, GROUP_Q_WIDTH), lambda bi, gi, qi, sl: (bi, qi, gi)),
            pl.BlockSpec((Q_BLOCK, LANES), lambda bi, gi, qi, sl: (bi * nq + qi, gi)),
        ],
        out_specs=pl.BlockSpec((1, Q_BLOCK, GROUP_Q_WIDTH), lambda bi, gi, qi, sl: (bi, qi, gi)),
    )
    return pl.pallas_call(
        functools.partial(_win_kernel, span=span),
        grid_spec=grid_spec,
        out_shape=jax.ShapeDtypeStruct((b, s, NSA_WIDTH), BF16),
        compiler_params=_params(("arbitrary", "arbitrary", "arbitrary")),
        name="win_attn_combine",
    )(slopes, proj3, proj3, proj3, o_cmp, o_slc, gates_raw)


def _matmul_res_kernel(*refs, n_pairs):
    r_ref = refs[2 * n_pairs]
    o_ref = refs[2 * n_pairs + 1]
    acc = r_ref[...]
    for i in range(n_pairs):
        acc = acc + jnp.dot(refs[i][...], refs[n_pairs + i][...], preferred_element_type=F32)
    o_ref[...] = acc


def _matmul_res(acts, weights, resid, tm, tn):
    m, n = resid.shape
    n_pairs = len(acts)
    in_specs = [pl.BlockSpec((tm, a.shape[1]), lambda i, j: (i, 0)) for a in acts]
    in_specs += [pl.BlockSpec((w.shape[0], tn), lambda i, j: (0, j)) for w in weights]
    in_specs += [pl.BlockSpec((tm, tn), lambda i, j: (i, j))]
    return pl.pallas_call(
        functools.partial(_matmul_res_kernel, n_pairs=n_pairs),
        grid=(m // tm, n // tn),
        in_specs=in_specs,
        out_specs=pl.BlockSpec((tm, tn), lambda i, j: (i, j)),
        out_shape=jax.ShapeDtypeStruct((m, n), F32),
        compiler_params=_params(("arbitrary", "arbitrary")),
        name="matmul_residual",
    )(*acts, *weights, resid)


def _ffn_kernel(h_ref, g_ref, wg_ref, wu_ref, wd_ref, gf_ref, o_ref, xn_ref, acc_ref, *, final_norm):
    k = pl.program_id(1)

    @pl.when(k == 0)
    def _():
        xn_ref[...] = _rms_scale(h_ref[...], g_ref[...]).astype(BF16)
        acc_ref[...] = jnp.zeros_like(acc_ref)

    xn = xn_ref[...]
    a = jnp.dot(xn, wg_ref[...], preferred_element_type=F32)
    u = jnp.dot(xn, wu_ref[...], preferred_element_type=F32)
    act = (a * jax.nn.sigmoid(a) * u).astype(BF16)
    acc_ref[...] += jnp.dot(act, wd_ref[...], preferred_element_type=F32)

    @pl.when(k == pl.num_programs(1) - 1)
    def _():
        out = h_ref[...] + acc_ref[...]
        if final_norm:
            out = _rms_scale(out, gf_ref[...])
        o_ref[...] = out


def _ffn(h, gain, w_gate, w_up, w_down, gain_final, final_norm, tm, tk):
    m, d = h.shape
    dff = w_gate.shape[1]
    return pl.pallas_call(
        functools.partial(_ffn_kernel, final_norm=final_norm),
        grid=(m // tm, dff // tk),
        in_specs=[
            pl.BlockSpec((tm, d), lambda i, k: (i, 0)),
            pl.BlockSpec((1, d), lambda i, k: (0, 0)),
            pl.BlockSpec((d, tk), lambda i, k: (0, k)),
            pl.BlockSpec((d, tk), lambda i, k: (0, k)),
            pl.BlockSpec((tk, d), lambda i, k: (k, 0)),
            pl.BlockSpec((1, d), lambda i, k: (0, 0)),
        ],
        out_specs=pl.BlockSpec((tm, d), lambda i, k: (i, 0)),
        out_shape=jax.ShapeDtypeStruct((m, d), F32),
        scratch_shapes=[pltpu.VMEM((tm, d), BF16), pltpu.VMEM((tm, d), F32)],
        compiler_params=_params(("arbitrary", "arbitrary")),
        name="ffn_swiglu",
    )(h, gain.reshape(1, d), w_gate, w_up, w_down, gain_final.reshape(1, d))


TAIL_ROWS = 8


def _conv_in_kernel(x_ref, g_ref, wb_ref, wc_ref, wz_ref, cw_ref, y_ref, xn_ref, tail_ref, *, tiles_per_seq):
    i = pl.program_id(0)
    j = pl.program_id(1)

    @pl.when(j == 0)
    def _():
        xn_ref[...] = _rms_scale(x_ref[...], g_ref[...]).astype(BF16)

    xn = xn_ref[...]
    bg = jnp.dot(xn, wb_ref[...], preferred_element_type=F32)
    z = jnp.dot(xn, wc_ref[...], preferred_element_type=F32) * jnp.dot(xn, wz_ref[...], preferred_element_type=F32)
    tm = z.shape[0]
    seq_start = (i % tiles_per_seq) == 0
    prev = jnp.where(seq_start, 0.0, tail_ref[j])
    tail_ref[j] = z[tm - TAIL_ROWS:, :]
    prev1 = prev[TAIL_ROWS - 1:TAIL_ROWS, :]
    prev2 = prev[TAIL_ROWS - 2:TAIL_ROWS - 1, :]
    row = lax.broadcasted_iota(jnp.int32, (tm, 1), 0)
    z1 = jnp.where(row >= 1, pltpu.roll(z, 1, 0), prev1)
    z2 = jnp.where(row >= 2, pltpu.roll(z, 2, 0), jnp.where(row == 1, prev1, prev2))
    zc = cw_ref[2:3, :] * z + cw_ref[1:2, :] * z1 + cw_ref[0:1, :] * z2
    y_ref[...] = (bg * zc).astype(y_ref.dtype)


def _conv_in(h, gain, w_in, conv_w, seq, tm, tn):
    m, d = h.shape
    cw = conv_w.shape[1]
    nj = cw // tn
    return pl.pallas_call(
        functools.partial(_conv_in_kernel, tiles_per_seq=seq // tm),
        grid=(m // tm, nj),
        in_specs=[
            pl.BlockSpec((tm, d), lambda i, j: (i, 0)),
            pl.BlockSpec((1, d), lambda i, j: (0, 0)),
            pl.BlockSpec((d, tn), lambda i, j: (0, j)),
            pl.BlockSpec((d, tn), lambda i, j: (0, nj + j)),
            pl.BlockSpec((d, tn), lambda i, j: (0, 2 * nj + j)),
            pl.BlockSpec((conv_w.shape[0], tn), lambda i, j: (0, j)),
        ],
        out_specs=pl.BlockSpec((tm, tn), lambda i, j: (i, j)),
        out_shape=jax.ShapeDtypeStruct((m, cw), BF16),
        scratch_shapes=[pltpu.VMEM((tm, d), BF16), pltpu.VMEM((nj, TAIL_ROWS, tn), F32)],
        compiler_params=_params(("arbitrary", "arbitrary")),
        name="conv_mixer_in",
    )(h, gain.reshape(1, d), w_in, w_in, w_in, conv_w)


def _gate_weight(w_in):
    per_group = NSA_REP * N_BRANCH
    cols = []
    for g in range(NSA_KV_GROUPS):
        blk = w_in[:, COL_GT + g * per_group:COL_GT + (g + 1) * per_group]
        cols.append(jnp.pad(blk, ((0, 0), (0, LANES - per_group))))
    return jnp.concatenate(cols, axis=1)


def _pick_tile(m, pref):
    t = min(m, pref)
    while m % t:
        t //= 2
    return t


def kernel(x, norm_mix, norm_ffn, norm_f, w_in_ab, w_out_ab, sgu_w, sgu_b, sgu_g, cmp_pe_k, cmp_w1_k, cmp_w2_k, cmp_pe_v, cmp_w1_v, cmp_w2_v, w_in_c, conv_w, w_out_c, w_gate, w_up, w_down):
    b, s, d = x.shape
    m = b * s
    tm = _pick_tile(m, 512)
    top_n = min(SLC_TOP_N, s // SLC_LEN)
    slopes = (2.0 ** (-8.0 * jnp.arange(1, NSA_HEADS + 1, dtype=F32) / NSA_HEADS)).astype(F32)
    imp_mat = jnp.asarray(_importance_matrix(s // CMP_STRIDE, s // SLC_LEN), BF16)

    h = x.reshape(m, d)

    w_in = w_in_ab[0]
    proj = _norm_matmul(h, norm_mix[0], w_in[:, :COL_GT].astype(BF16), BF16, tm, 512)
    gates_raw = _norm_matmul(h, norm_mix[0], _gate_weight(w_in).astype(BF16), F32, tm, LANES * NSA_KV_GROUPS)
    y_a = _sgu(proj, sgu_w[0], sgu_b[0], sgu_g[0], tm)
    proj3 = proj.reshape(b, s, COL_GT)
    cmp_kv = _compress(
        proj3,
        jnp.stack([cmp_pe_k[0], cmp_pe_v[0]]),
        jnp.stack([cmp_w1_k[0], cmp_w1_v[0]]).astype(BF16),
        jnp.stack([cmp_w2_k[0], cmp_w2_v[0]]).astype(BF16),
    )
    o_cmp, idx = _cmp_select(proj3, cmp_kv, slopes, imp_mat, top_n)
    o_slc = _slc_attention(proj3, idx, slopes, top_n)
    y_b = _win_combine(proj3, o_cmp, o_slc, gates_raw, slopes)
    w_out = w_out_ab[0].astype(BF16)
    h = _matmul_res([y_a, y_b.reshape(m, NSA_WIDTH)], [w_out[:SGU_WIDTH], w_out[SGU_WIDTH:]], h, tm, 512)
    h = _ffn(h, norm_ffn[0], w_gate[0].astype(BF16), w_up[0].astype(BF16), w_down[0].astype(BF16), norm_f, False, tm, 512)

    y_c = _conv_in(h, norm_mix[1], w_in_c[0].astype(BF16), conv_w[0], s, tm, 512)
    h = _matmul_res([y_c], [w_out_c[0].astype(BF16)], h, tm, 512)
    h = _ffn(h, norm_ffn[1], w_gate[1].astype(BF16), w_up[1].astype(BF16), w_down[1].astype(BF16), norm_f, True, tm, 512)
    return h.reshape(b, s, d)
```

```python
import functools
import math

import jax
import jax.numpy as jnp
import numpy as np
from jax import lax
from jax.experimental import pallas as pl
from jax.experimental.pallas import tpu as pltpu

F32 = jnp.float32
BF16 = jnp.bfloat16

HEAD_DIM = 128
SGU_GROUPS = 8
SGU_CHUNK = 128
SGU_WIDTH = SGU_GROUPS * HEAD_DIM
NSA_HEADS = 8
NSA_KV_GROUPS = 2
NSA_REP = NSA_HEADS // NSA_KV_GROUPS
NSA_WIDTH = NSA_HEADS * HEAD_DIM
KV_WIDTH = NSA_KV_GROUPS * HEAD_DIM
GROUP_Q_WIDTH = NSA_REP * HEAD_DIM
CMP_LEN = 32
CMP_STRIDE = 16
SLC_LEN = 64
SLC_TOP_N = 16
WINDOW = 512
Q_BLOCK = 128
N_BRANCH = 3
FORCE_SCORE = 1e9
EPS = 1e-6
NEG = -1e30
REMOVED = -3e38
ATTN_SCALE = HEAD_DIM ** -0.5

COL_U = 0
COL_V = SGU_WIDTH
COL_Q = 2 * SGU_WIDTH
COL_KC = COL_Q + NSA_WIDTH
COL_VC = COL_KC + KV_WIDTH
COL_KS = COL_VC + KV_WIDTH
COL_VS = COL_KS + KV_WIDTH
COL_KW = COL_VS + KV_WIDTH
COL_VW = COL_KW + KV_WIDTH
COL_GT = COL_VW + KV_WIDTH

LANES = 128
VMEM_LIMIT = 56 * 1024 * 1024


def _params(sem, vmem=VMEM_LIMIT):
    return pltpu.CompilerParams(dimension_semantics=sem, vmem_limit_bytes=vmem)


def _rms_scale(x, gain_row):
    ms = jnp.mean(x * x, axis=-1, keepdims=True)
    return x * lax.rsqrt(ms + EPS) * gain_row


def _masked_softmax(s, mask):
    s = jnp.where(mask, s, NEG)
    m = jnp.max(s, axis=-1, keepdims=True)
    p = jnp.exp(s - m) * mask.astype(F32)
    denom = jnp.maximum(jnp.sum(p, axis=-1, keepdims=True), 1e-30)
    return p * (1.0 / denom)


def _dot_nt(a, b):
    return lax.dot_general(a, b, (((1,), (1,)), ((), ())), preferred_element_type=F32)


def _norm_matmul_kernel(x_ref, g_ref, w_ref, o_ref, xn_ref):
    @pl.when(pl.program_id(1) == 0)
    def _():
        xn_ref[...] = _rms_scale(x_ref[...], g_ref[...]).astype(BF16)

    o_ref[...] = jnp.dot(xn_ref[...], w_ref[...], preferred_element_type=F32).astype(o_ref.dtype)


def _norm_matmul(x, gain, w, out_dtype, tm, tn):
    m, d = x.shape
    n = w.shape[1]
    return pl.pallas_call(
        _norm_matmul_kernel,
        grid=(m // tm, n // tn),
        in_specs=[
            pl.BlockSpec((tm, d), lambda i, j: (i, 0)),
            pl.BlockSpec((1, d), lambda i, j: (0, 0)),
            pl.BlockSpec((d, tn), lambda i, j: (0, j)),
        ],
        out_specs=pl.BlockSpec((tm, tn), lambda i, j: (i, j)),
        out_shape=jax.ShapeDtypeStruct((m, n), out_dtype),
        scratch_shapes=[pltpu.VMEM((tm, d), BF16)],
        compiler_params=_params(("arbitrary", "arbitrary")),
        name="norm_matmul",
    )(x, gain.reshape(1, d), w)


def _sgu_kernel(u_ref, v_ref, w_ref, bt_ref, gain_ref, o_ref, *, chunks):
    row = lax.broadcasted_iota(jnp.int32, (SGU_CHUNK, SGU_CHUNK), 0)
    col = lax.broadcasted_iota(jnp.int32, (SGU_CHUNK, SGU_CHUNK), 1)
    tri = row >= col
    for g in range(SGU_GROUPS):
        w = jnp.where(tri, w_ref[g], 0.0).astype(BF16)
        bias = bt_ref[:, g:g + 1]
        gain = gain_ref[g:g + 1, :]
        cs = slice(g * HEAD_DIM, (g + 1) * HEAD_DIM)
        for c in range(chunks):
            rs = slice(c * SGU_CHUNK, (c + 1) * SGU_CHUNK)
            u = jax.nn.gelu(u_ref[rs, cs].astype(F32))
            v = _rms_scale(jax.nn.gelu(v_ref[rs, cs].astype(F32)), gain)
            sg = jnp.dot(w, v.astype(BF16), preferred_element_type=F32) + bias
            o_ref[rs, cs] = (u * sg).astype(o_ref.dtype)


def _sgu(proj, sgu_w, sgu_b, sgu_g, tr):
    m = proj.shape[0]
    chunks = tr // SGU_CHUNK
    return pl.pallas_call(
        functools.partial(_sgu_kernel, chunks=chunks),
        grid=(m // tr,),
        in_specs=[
            pl.BlockSpec((tr, SGU_WIDTH), lambda i: (i, COL_U // SGU_WIDTH)),
            pl.BlockSpec((tr, SGU_WIDTH), lambda i: (i, COL_V // SGU_WIDTH)),
            pl.BlockSpec((SGU_GROUPS, SGU_CHUNK, SGU_CHUNK), lambda i: (0, 0, 0)),
            pl.BlockSpec((SGU_CHUNK, SGU_GROUPS), lambda i: (0, 0)),
            pl.BlockSpec((SGU_GROUPS, HEAD_DIM), lambda i: (0, 0)),
        ],
        out_specs=pl.BlockSpec((tr, SGU_WIDTH), lambda i: (i, 0)),
        out_shape=jax.ShapeDtypeStruct((m, SGU_WIDTH), BF16),
        compiler_params=_params(("arbitrary",)),
        name="sgu_mixer",
    )(proj, proj, sgu_w, sgu_b.T, sgu_g)


def _compress_kernel(x_ref, pe_ref, w1_ref, w2_ref, o_ref, xf_ref, *, seq):
    nc = seq // CMP_STRIDE
    xf_ref[0:seq, :] = x_ref[0].astype(F32)
    xf_ref[seq:seq + CMP_STRIDE, :] = jnp.zeros((CMP_STRIDE, HEAD_DIM), F32)
    acc = jnp.zeros((nc, HEAD_DIM), F32)
    for l in range(CMP_LEN):
        rows = xf_ref[pl.ds(l, nc, stride=CMP_STRIDE), :] + pe_ref[0, l:l + 1, :]
        acc = acc + jnp.dot(rows.astype(BF16), w1_ref[0, l], preferred_element_type=F32)
    hid = jax.nn.gelu(acc)
    o_ref[0, 0] = jnp.dot(hid.astype(BF16), w2_ref[0], preferred_element_type=F32).astype(o_ref.dtype)


def _compress(proj3, pe, w1, w2):
    b, s, _ = proj3.shape
    nc = s // CMP_STRIDE
    g = NSA_KV_GROUPS
    col0 = COL_KC // HEAD_DIM
    return pl.pallas_call(
        functools.partial(_compress_kernel, seq=s),
        grid=(2, b, g),
        in_specs=[
            pl.BlockSpec((1, s, HEAD_DIM), lambda kv, bi, gi: (bi, 0, col0 + g * kv + gi)),
            pl.BlockSpec((1, CMP_LEN, HEAD_DIM), lambda kv, bi, gi: (kv, 0, 0)),
            pl.BlockSpec((1, CMP_LEN, HEAD_DIM, HEAD_DIM), lambda kv, bi, gi: (kv, 0, 0, 0)),
            pl.BlockSpec((1, HEAD_DIM, HEAD_DIM), lambda kv, bi, gi: (kv, 0, 0)),
        ],
        out_specs=pl.BlockSpec((1, 1, nc, HEAD_DIM), lambda kv, bi, gi: (kv, bi * g + gi, 0, 0)),
        out_shape=jax.ShapeDtypeStruct((2, b * g, nc, HEAD_DIM), BF16),
        scratch_shapes=[pltpu.VMEM((s + CMP_STRIDE, HEAD_DIM), F32)],
        compiler_params=_params(("arbitrary", "arbitrary", "arbitrary")),
        name="kv_compress",
    )(proj3, pe, w1, w2)


def _importance_matrix(nc, n_slc):
    r1 = SLC_LEN // CMP_STRIDE
    r2 = CMP_LEN // CMP_STRIDE
    mat = np.zeros((nc, n_slc), np.float32)
    for j in range(n_slc):
        for m_ in range(r1):
            for n_ in range(r2):
                n = r1 * j + m_ - n_
                if 0 <= n < nc:
                    mat[n, j] += 1.0
    return mat


def _cmp_select_kernel(slopes_ref, q_ref, kc_ref, vc_ref, m_ref, oc_ref, idx_ref, *, nc, n_slc, top_n):
    g = pl.program_id(1)
    q0 = pl.program_id(2) * Q_BLOCK
    t = q0 + lax.broadcasted_iota(jnp.int32, (Q_BLOCK, 1), 0)
    n_ix = lax.broadcasted_iota(jnp.int32, (1, nc), 1)
    valid = (n_ix * CMP_STRIDE + (CMP_LEN - 1)) <= t
    dist = t.astype(F32) - ((n_ix * CMP_STRIDE).astype(F32) + (CMP_LEN - 1) / 2)
    kc = kc_ref[0, 0]
    vc = vc_ref[0, 0]
    psum = jnp.zeros((Q_BLOCK, nc), F32)
    for h in range(NSA_REP):
        cs = slice(h * HEAD_DIM, (h + 1) * HEAD_DIM)
        slope = slopes_ref[g * NSA_REP + h]
        s = _dot_nt(q_ref[0, :, cs], kc) * ATTN_SCALE - slope * dist
        p = _masked_softmax(s, valid)
        psum = psum + p
        oc_ref[0, :, cs] = jnp.dot(p.astype(BF16), vc, preferred_element_type=F32).astype(oc_ref.dtype)
    hi = psum.astype(BF16)
    lo = (psum - hi.astype(F32)).astype(BF16)
    mat = m_ref[...]
    imp = jnp.dot(hi, mat, preferred_element_type=F32) + jnp.dot(lo, mat, preferred_element_type=F32)
    j_ix = lax.broadcasted_iota(jnp.int32, (1, n_slc), 1)
    cur = t // SLC_LEN
    causal = j_ix <= cur
    forced = (j_ix == 0) | (j_ix == cur) | (j_ix == cur - 1)
    score = jnp.where(forced, FORCE_SCORE, jnp.where(causal, imp, -1.0))
    j_f = j_ix.astype(F32)
    r_ix = lax.broadcasted_iota(jnp.int32, (1, top_n), 1)
    idx = jnp.zeros((Q_BLOCK, top_n), jnp.int32)
    for r in range(top_n):
        best = jnp.max(score, axis=-1, keepdims=True)
        pick = jnp.min(jnp.where(score == best, j_f, float(n_slc)), axis=-1, keepdims=True)
        score = jnp.where(j_f == pick, REMOVED, score)
        idx = jnp.where(r_ix == r, pick.astype(jnp.int32), idx)
    idx_ref[0, 0] = idx


def _cmp_select(proj3, cmp_kv, slopes, imp_mat, top_n):
    b, s, _ = proj3.shape
    g = NSA_KV_GROUPS
    nc = s // CMP_STRIDE
    n_slc = s // SLC_LEN
    grid_spec = pltpu.PrefetchScalarGridSpec(
        num_scalar_prefetch=1,
        grid=(b, g, s // Q_BLOCK),
        in_specs=[
            pl.BlockSpec((1, Q_BLOCK, GROUP_Q_WIDTH), lambda bi, gi, qi, sl: (bi, qi, COL_Q // GROUP_Q_WIDTH + gi)),
            pl.BlockSpec((1, 1, nc, HEAD_DIM), lambda bi, gi, qi, sl: (0, bi * g + gi, 0, 0)),
            pl.BlockSpec((1, 1, nc, HEAD_DIM), lambda bi, gi, qi, sl: (1, bi * g + gi, 0, 0)),
            pl.BlockSpec((nc, n_slc), lambda bi, gi, qi, sl: (0, 0)),
        ],
        out_specs=[
            pl.BlockSpec((1, Q_BLOCK, GROUP_Q_WIDTH), lambda bi, gi, qi, sl: (bi, qi, gi)),
            pl.BlockSpec((1, 1, Q_BLOCK, top_n), lambda bi, gi, qi, sl: (bi, gi, qi, 0)),
        ],
    )
    return pl.pallas_call(
        functools.partial(_cmp_select_kernel, nc=nc, n_slc=n_slc, top_n=top_n),
        grid_spec=grid_spec,
        out_shape=[
            jax.ShapeDtypeStruct((b, s, NSA_WIDTH), F32),
            jax.ShapeDtypeStruct((b, g, s, top_n), jnp.int32),
        ],
        compiler_params=_params(("arbitrary", "arbitrary", "arbitrary")),
        name="cmp_attn_select",
    )(slopes, proj3, cmp_kv, cmp_kv, imp_mat)


KT_ROWS = HEAD_DIM + 16
POS_HI = float(LANES)
Q_ROWS = 8
GRP = 4


def _slc_kernel(slopes_ref, idx_ref, q_ref, k_ref, v_ref, o_ref, qs_ref, os_ref, kt_ref, s_buf, e_buf, *, top_n, seq):
    g = pl.program_id(1)
    q0 = pl.program_id(2) * Q_BLOCK
    n_tiles = top_n * SLC_LEN // LANES
    n_grp = Q_BLOCK // GRP

    @pl.when(pl.program_id(2) == 0)
    def _():
        sub = lax.broadcasted_iota(jnp.int32, (KT_ROWS - HEAD_DIM, LANES), 0)
        lane_f = lax.broadcasted_iota(jnp.int32, (KT_ROWS - HEAD_DIM, LANES), 1).astype(F32)

        def tr_body(i, carry):
            r0 = pl.multiple_of(i * LANES, LANES)
            tile = k_ref[0, pl.ds(r0, LANES), :].astype(F32).T
            pos_rows = jnp.where(sub == 0, i.astype(F32), jnp.where(sub == 1, lane_f, 0.0))
            ext = jnp.concatenate([tile, pos_rows], axis=0)
            kt_ref[2 * i] = ext.astype(BF16)
            kt_ref[2 * i + 1] = pltpu.roll(ext, SLC_LEN, 1).astype(BF16)
            return carry
        lax.fori_loop(0, seq // LANES, tr_body, 0)
        s_buf[...] = jnp.zeros_like(s_buf)
        e_buf[...] = jnp.zeros_like(e_buf)
        qs_ref[...] = jnp.zeros_like(qs_ref)

    for h in range(NSA_REP):
        qs_ref[pl.ds(h, Q_BLOCK, stride=Q_ROWS), :] = q_ref[0, :, h * HEAD_DIM:(h + 1) * HEAD_DIM].astype(F32)
    row = lax.broadcasted_iota(jnp.int32, (Q_ROWS, 1), 0)
    low_rows = row < NSA_REP
    e_row = lax.broadcasted_iota(jnp.int32, (Q_ROWS, KT_ROWS - HEAD_DIM), 0)
    e_col = lax.broadcasted_iota(jnp.int32, (Q_ROWS, KT_ROWS - HEAD_DIM), 1)
    slope_rows = []
    pos_cols = []
    pos_row = [NSA_REP, 0]
    for par in range(2):
        sl = jnp.zeros((Q_ROWS, 1), F32)
        for h in range(NSA_REP):
            sl = jnp.where(row == h + par * NSA_REP, slopes_ref[g * NSA_REP + h], sl)
        slope_rows.append(sl)
        pos_cols.append(jnp.where((e_row == pos_row[par]) & (e_col == 0), POS_HI,
                                  jnp.where((e_row == pos_row[par]) & (e_col == 1), 1.0, 0.0)).astype(BF16))
    low_half = lax.broadcasted_iota(jnp.int32, (KT_ROWS, LANES), 1) < SLC_LEN

    def block_ids(tloc):
        return [idx_ref[0, 0, tloc, r] for r in range(top_n)]

    def stage_scores(grp, slot):
        for qi in range(GRP):
            par = qi % 2
            tloc = grp * GRP + qi
            ids = block_ids(tloc)
            q_rows = qs_ref[pl.ds(pl.multiple_of(tloc * Q_ROWS, Q_ROWS), Q_ROWS), :]
            if par:
                q_rows = pltpu.roll(q_rows, NSA_REP, 0)
            lhs = jnp.concatenate([q_rows.astype(BF16), pos_cols[par]], axis=1)
            tiles = [jnp.where(low_half, kt_ref[ids[2 * c]], kt_ref[ids[2 * c + 1] ^ 1]) for c in range(n_tiles)]
            s_buf[slot, qi] = jnp.dot(lhs, jnp.concatenate(tiles, axis=1), preferred_element_type=F32)

    def stage_softmax(grp, slot):
        for qi in range(GRP):
            par = qi % 2
            t_f = (q0 + grp * GRP + qi).astype(F32)
            raw = s_buf[slot, qi]
            key_pos = raw[pos_row[par]:pos_row[par] + 1, :]
            mask = key_pos <= t_f
            s = raw * ATTN_SCALE - slope_rows[par] * (t_f - key_pos)
            e_buf[slot, qi] = _masked_softmax(s, mask)

    def stage_values(grp, slot):
        for w in range(GRP // 2):
            tloc = grp * GRP + 2 * w
            v_pair = []
            for par in range(2):
                v_pair.append(jnp.concatenate([v_ref[0, j] for j in block_ids(tloc + par)], axis=0))
            prob = jnp.where(low_rows, e_buf[slot, 2 * w], e_buf[slot, 2 * w + 1]).astype(BF16)
            acc = jnp.dot(prob, jnp.concatenate(v_pair, axis=1), preferred_element_type=F32)
            os_ref[pl.ds(pl.multiple_of(tloc * NSA_REP, 2 * NSA_REP), 2 * NSA_REP), :] = jnp.where(
                low_rows, acc[:, :HEAD_DIM], acc[:, HEAD_DIM:])

    def body(it, carry):
        cur = it % 2
        stage_values(jnp.clip(it - 2, 0, n_grp - 1), cur)
        stage_softmax(jnp.clip(it - 1, 0, n_grp - 1), 1 - cur)
        stage_scores(jnp.minimum(it, n_grp - 1), cur)
        return carry

    lax.fori_loop(0, n_grp + 2, body, 0)
    for h in range(NSA_REP):
        o_ref[0, :, h * HEAD_DIM:(h + 1) * HEAD_DIM] = os_ref[pl.ds(h, Q_BLOCK, stride=NSA_REP), :].astype(o_ref.dtype)


def _slc_attention(proj3, idx, slopes, top_n):
    b, s, width = proj3.shape
    g = NSA_KV_GROUPS
    n_slc = s // SLC_LEN
    assert top_n % (LANES // SLC_LEN) == 0 and Q_BLOCK % GRP == 0 and GRP % 2 == 0
    proj4 = proj3.reshape(b, n_slc, SLC_LEN, width)
    grid_spec = pltpu.PrefetchScalarGridSpec(
        num_scalar_prefetch=1,
        grid=(b, g, s // Q_BLOCK),
        in_specs=[
            pl.BlockSpec((1, 1, Q_BLOCK, top_n), lambda bi, gi, qi, sl: (bi, gi, qi, 0), memory_space=pltpu.SMEM),
            pl.BlockSpec((1, Q_BLOCK, GROUP_Q_WIDTH), lambda bi, gi, qi, sl: (bi, qi, COL_Q // GROUP_Q_WIDTH + gi)),
            pl.BlockSpec((1, s, HEAD_DIM), lambda bi, gi, qi, sl: (bi, 0, COL_KS // HEAD_DIM + gi)),
            pl.BlockSpec((1, n_slc, SLC_LEN, HEAD_DIM), lambda bi, gi, qi, sl: (bi, 0, 0, COL_VS // HEAD_DIM + gi)),
        ],
        out_specs=pl.BlockSpec((1, Q_BLOCK, GROUP_Q_WIDTH), lambda bi, gi, qi, sl: (bi, qi, gi)),
        scratch_shapes=[
            pltpu.VMEM((Q_BLOCK * Q_ROWS, HEAD_DIM), F32),
            pltpu.VMEM((Q_BLOCK * NSA_REP, HEAD_DIM), F32),
            pltpu.VMEM((n_slc, KT_ROWS, LANES), BF16),
            pltpu.VMEM((2, GRP, Q_ROWS, top_n * SLC_LEN), F32),
            pltpu.VMEM((2, GRP, Q_ROWS, top_n * SLC_LEN), F32),
        ],
    )
    return pl.pallas_call(
        functools.partial(_slc_kernel, top_n=top_n, seq=s),
        grid_spec=grid_spec,
        out_shape=jax.ShapeDtypeStruct((b, s, NSA_WIDTH), F32),
        compiler_params=_params(("arbitrary", "arbitrary", "arbitrary")),
        name="slc_attn",
    )(slopes, idx, proj3, proj3, proj4)


def _win_kernel(slopes_ref, q_ref, k_ref, v_ref, oc_ref, os_ref, gt_ref, y_ref, *, span):
    g = pl.program_id(1)
    q0 = pl.program_id(2) * Q_BLOCK
    start = pl.multiple_of(jnp.maximum(q0 - WINDOW, 0), Q_BLOCK)
    kw = k_ref[0, pl.ds(start, span), :]
    vw = v_ref[0, pl.ds(start, span), :]
    t = q0 + lax.broadcasted_iota(jnp.int32, (Q_BLOCK, 1), 0)
    s_pos = start + lax.broadcasted_iota(jnp.int32, (1, span), 1)
    rel = t - s_pos
    mask = (rel >= 0) & (rel < WINDOW)
    rel_f = rel.astype(F32)
    gates = jax.nn.sigmoid(gt_ref[...])
    for h in range(NSA_REP):
        cs = slice(h * HEAD_DIM, (h + 1) * HEAD_DIM)
        slope = slopes_ref[g * NSA_REP + h]
        s = _dot_nt(q_ref[0, :, cs], kw) * ATTN_SCALE - slope * rel_f
        p = _masked_softmax(s, mask)
        ow = jnp.dot(p.astype(BF16), vw, preferred_element_type=F32)
        c = h * N_BRANCH
        y = gates[:, c:c + 1] * oc_ref[0, :, cs] + gates[:, c + 1:c + 2] * os_ref[0, :, cs] + gates[:, c + 2:c + 3] * ow
        y_ref[0, :, cs] = y.astype(y_ref.dtype)


def _win_combine(proj3, o_cmp, o_slc, gates_raw, slopes):
    b, s, _ = proj3.shape
    g = NSA_KV_GROUPS
    nq = s // Q_BLOCK
    span = min(WINDOW + Q_BLOCK, s)
    grid_spec = pltpu.PrefetchScalarGridSpec(
        num_scalar_prefetch=1,
        grid=(b, g, nq),
        in_specs=[
            pl.BlockSpec((1, Q_BLOCK, GROUP_Q_WIDTH), lambda bi, gi, qi, sl: (bi, qi, COL_Q // GROUP_Q_WIDTH + gi)),
            pl.BlockSpec((1, s, HEAD_DIM), lambda bi, gi, qi, sl: (bi, 0, COL_KW // HEAD_DIM + gi)),
            pl.BlockSpec((1, s, HEAD_DIM), lambda bi, gi, qi, sl: (bi, 0, COL_VW // HEAD_DIM + gi)),
            pl.BlockSpec((1, Q_BLOCK, GROUP_Q_WIDTH), lambda bi, gi, qi, sl: (bi, qi, gi)),
            pl.BlockSpec((1, Q_BLOCK, GROUP_Q_WIDTH), lambda bi, gi, qi, sl: (bi, qi, gi)),
            pl.BlockSpec((Q_BLOCK, LANES), lambda bi, gi, qi, sl: (bi * nq + qi, gi)),
        ],
        out_specs=pl.BlockSpec((1, Q_BLOCK, GROUP_Q_WIDTH), lambda bi, gi, qi, sl: (bi, qi, gi)),
    )
    return pl.pallas_call(
        functools.partial(_win_kernel, span=span),
        grid_spec=grid_spec,
        out_shape=jax.ShapeDtypeStruct((b, s, NSA_WIDTH), BF16),
        compiler_params=_params(("arbitrary", "arbitrary", "arbitrary")),
        name="win_attn_combine",
    )(slopes, proj3, proj3, proj3, o_cmp, o_slc, gates_raw)


def _matmul_res_kernel(*refs, n_pairs):
    r_ref = refs[2 * n_pairs]
    o_ref = refs[2 * n_pairs + 1]
    acc = r_ref[...]
    for i in range(n_pairs):
        acc = acc + jnp.dot(refs[i][...], refs[n_pairs + i][...], preferred_element_type=F32)
    o_ref[...] = acc


def _matmul_res(acts, weights, resid, tm, tn):
    m, n = resid.shape
    n_pairs = len(acts)
    in_specs = [pl.BlockSpec((tm, a.shape[1]), lambda i, j: (i, 0)) for a in acts]
    in_specs += [pl.BlockSpec((w.shape[0], tn), lambda i, j: (0, j)) for w in weights]
    in_specs += [pl.BlockSpec((tm, tn), lambda i, j: (i, j))]
    return pl.pallas_call(
        functools.partial(_matmul_res_kernel, n_pairs=n_pairs),
        grid=(m // tm, n // tn),
        in_specs=in_specs,
        out_specs=pl.BlockSpec((tm, tn), lambda i, j: (i, j)),
        out_shape=jax.ShapeDtypeStruct((m, n), F32),
        compiler_params=_params(("arbitrary", "arbitrary")),
        name="matmul_residual",
    )(*acts, *weights, resid)


def _ffn_kernel(h_ref, g_ref, wg_ref, wu_ref, wd_ref, gf_ref, o_ref, xn_ref, acc_ref, *, final_norm):
    k = pl.program_id(1)

    @pl.when(k == 0)
    def _():
        xn_ref[...] = _rms_scale(h_ref[...], g_ref[...]).astype(BF16)
        acc_ref[...] = jnp.zeros_like(acc_ref)

    xn = xn_ref[...]
    a = jnp.dot(xn, wg_ref[...], preferred_element_type=F32)
    u = jnp.dot(xn, wu_ref[...], preferred_element_type=F32)
    act = (a * jax.nn.sigmoid(a) * u).astype(BF16)
    acc_ref[...] += jnp.dot(act, wd_ref[...], preferred_element_type=F32)

    @pl.when(k == pl.num_programs(1) - 1)
    def _():
        out = h_ref[...] + acc_ref[...]
        if final_norm:
            out = _rms_scale(out, gf_ref[...])
        o_ref[...] = out


def _ffn(h, gain, w_gate, w_up, w_down, gain_final, final_norm, tm, tk):
    m, d = h.shape
    dff = w_gate.shape[1]
    return pl.pallas_call(
        functools.partial(_ffn_kernel, final_norm=final_norm),
        grid=(m // tm, dff // tk),
        in_specs=[
            pl.BlockSpec((tm, d), lambda i, k: (i, 0)),
            pl.BlockSpec((1, d), lambda i, k: (0, 0)),
            pl.BlockSpec((d, tk), lambda i, k: (0, k)),
            pl.BlockSpec((d, tk), lambda i, k: (0, k)),
            pl.BlockSpec((tk, d), lambda i, k: (k, 0)),
            pl.BlockSpec((1, d), lambda i, k: (0, 0)),
        ],
        out_specs=pl.BlockSpec((tm, d), lambda i, k: (i, 0)),
        out_shape=jax.ShapeDtypeStruct((m, d), F32),
        scratch_shapes=[pltpu.VMEM((tm, d), BF16), pltpu.VMEM((tm, d), F32)],
        compiler_params=_params(("arbitrary", "arbitrary")),
        name="ffn_swiglu",
    )(h, gain.reshape(1, d), w_gate, w_up, w_down, gain_final.reshape(1, d))


TAIL_ROWS = 8


def _conv_in_kernel(x_ref, g_ref, wb_ref, wc_ref, wz_ref, cw_ref, y_ref, xn_ref, tail_ref, *, tiles_per_seq):
    i = pl.program_id(0)
    j = pl.program_id(1)

    @pl.when(j == 0)
    def _():
        xn_ref[...] = _rms_scale(x_ref[...], g_ref[...]).astype(BF16)

    xn = xn_ref[...]
    bg = jnp.dot(xn, wb_ref[...], preferred_element_type=F32)
    z = jnp.dot(xn, wc_ref[...], preferred_element_type=F32) * jnp.dot(xn, wz_ref[...], preferred_element_type=F32)
    tm = z.shape[0]
    seq_start = (i % tiles_per_seq) == 0
    prev = jnp.where(seq_start, 0.0, tail_ref[j])
    tail_ref[j] = z[tm - TAIL_ROWS:, :]
    prev1 = prev[TAIL_ROWS - 1:TAIL_ROWS, :]
    prev2 = prev[TAIL_ROWS - 2:TAIL_ROWS - 1, :]
    row = lax.broadcasted_iota(jnp.int32, (tm, 1), 0)
    z1 = jnp.where(row >= 1, pltpu.roll(z, 1, 0), prev1)
    z2 = jnp.where(row >= 2, pltpu.roll(z, 2, 0), jnp.where(row == 1, prev1, prev2))
    zc = cw_ref[2:3, :] * z + cw_ref[1:2, :] * z1 + cw_ref[0:1, :] * z2
    y_ref[...] = (bg * zc).astype(y_ref.dtype)


def _conv_in(h, gain, w_in, conv_w, seq, tm, tn):
    m, d = h.shape
    cw = conv_w.shape[1]
    nj = cw // tn
    return pl.pallas_call(
        functools.partial(_conv_in_kernel, tiles_per_seq=seq // tm),
        grid=(m // tm, nj),
        in_specs=[
            pl.BlockSpec((tm, d), lambda i, j: (i, 0)),
            pl.BlockSpec((1, d), lambda i, j: (0, 0)),
            pl.BlockSpec((d, tn), lambda i, j: (0, j)),
            pl.BlockSpec((d, tn), lambda i, j: (0, nj + j)),
            pl.BlockSpec((d, tn), lambda i, j: (0, 2 * nj + j)),
            pl.BlockSpec((conv_w.shape[0], tn), lambda i, j: (0, j)),
        ],
        out_specs=pl.BlockSpec((tm, tn), lambda i, j: (i, j)),
        out_shape=jax.ShapeDtypeStruct((m, cw), BF16),
        scratch_shapes=[pltpu.VMEM((tm, d), BF16), pltpu.VMEM((nj, TAIL_ROWS, tn), F32)],
        compiler_params=_params(("arbitrary", "arbitrary")),
        name="conv_mixer_in",
    )(h, gain.reshape(1, d), w_in, w_in, w_in, conv_w)


def _gate_weight(w_in):
    per_group = NSA_REP * N_BRANCH
    cols = []
    for g in range(NSA_KV_GROUPS):
        blk = w_in[:, COL_GT + g * per_group:COL_GT + (g + 1) * per_group]
        cols.append(jnp.pad(blk, ((0, 0), (0, LANES - per_group))))
    return jnp.concatenate(cols, axis=1)


def _pick_tile(m, pref):
    t = min(m, pref)
    while m % t:
        t //= 2
    return t


def kernel(x, norm_mix, norm_ffn, norm_f, w_in_ab, w_out_ab, sgu_w, sgu_b, sgu_g, cmp_pe_k, cmp_w1_k, cmp_w2_k, cmp_pe_v, cmp_w1_v, cmp_w2_v, w_in_c, conv_w, w_out_c, w_gate, w_up, w_down):
    b, s, d = x.shape
    m = b * s
    tm = _pick_tile(m, 512)
    top_n = min(SLC_TOP_N, s // SLC_LEN)
    slopes = (2.0 ** (-8.0 * jnp.arange(1, NSA_HEADS + 1, dtype=F32) / NSA_HEADS)).astype(F32)
    imp_mat = jnp.asarray(_importance_matrix(s // CMP_STRIDE, s // SLC_LEN), BF16)

    h = x.reshape(m, d)

    w_in = w_in_ab[0]
    proj = _norm_matmul(h, norm_mix[0], w_in[:, :COL_GT].astype(BF16), BF16, tm, 512)
    gates_raw = _norm_matmul(h, norm_mix[0], _gate_weight(w_in).astype(BF16), F32, tm, LANES * NSA_KV_GROUPS)
    y_a = _sgu(proj, sgu_w[0], sgu_b[0], sgu_g[0], tm)
    proj3 = proj.reshape(b, s, COL_GT)
    cmp_kv = _compress(
        proj3,
        jnp.stack([cmp_pe_k[0], cmp_pe_v[0]]),
        jnp.stack([cmp_w1_k[0], cmp_w1_v[0]]).astype(BF16),
        jnp.stack([cmp_w2_k[0], cmp_w2_v[0]]).astype(BF16),
    )
    o_cmp, idx = _cmp_select(proj3, cmp_kv, slopes, imp_mat, top_n)
    o_slc = _slc_attention(proj3, idx, slopes, top_n)
    y_b = _win_combine(proj3, o_cmp, o_slc, gates_raw, slopes)
    w_out = w_out_ab[0].astype(BF16)
    h = _matmul_res([y_a, y_b.reshape(m, NSA_WIDTH)], [w_out[:SGU_WIDTH], w_out[SGU_WIDTH:]], h, tm, 512)
    h = _ffn(h, norm_ffn[0], w_gate[0].astype(BF16), w_up[0].astype(BF16), w_down[0].astype(BF16), norm_f, False, tm, 512)

    y_c = _conv_in(h, norm_mix[1], w_in_c[0].astype(BF16), conv_w[0], s, tm, 512)
    h = _matmul_res([y_c], [w_out_c[0].astype(BF16)], h, tm, 512)
    h = _ffn(h, norm_ffn[1], w_gate[1].astype(BF16), w_up[1].astype(BF16), w_down[1].astype(BF16), norm_f, True, tm, 512)
    return h.reshape(b, s, d)
```

```python
import functools
import math

import jax
import jax.numpy as jnp
import numpy as np
from jax import lax
from jax.experimental import pallas as pl
from jax.experimental.pallas import tpu as pltpu

F32 = jnp.float32
BF16 = jnp.bfloat16

HEAD_DIM = 128
SGU_GROUPS = 8
SGU_CHUNK = 128
SGU_WIDTH = SGU_GROUPS * HEAD_DIM
NSA_HEADS = 8
NSA_KV_GROUPS = 2
NSA_REP = NSA_HEADS // NSA_KV_GROUPS
NSA_WIDTH = NSA_HEADS * HEAD_DIM
KV_WIDTH = NSA_KV_GROUPS * HEAD_DIM
GROUP_Q_WIDTH = NSA_REP * HEAD_DIM
CMP_LEN = 32
CMP_STRIDE = 16
SLC_LEN = 64
SLC_TOP_N = 16
WINDOW = 512
Q_BLOCK = 128
N_BRANCH = 3
FORCE_SCORE = 1e9
EPS = 1e-6
NEG = -1e30
REMOVED = -3e38
ATTN_SCALE = HEAD_DIM ** -0.5

COL_U = 0
COL_V = SGU_WIDTH
COL_Q = 2 * SGU_WIDTH
COL_KC = COL_Q + NSA_WIDTH
COL_VC = COL_KC + KV_WIDTH
COL_KS = COL_VC + KV_WIDTH
COL_VS = COL_KS + KV_WIDTH
COL_KW = COL_VS + KV_WIDTH
COL_VW = COL_KW + KV_WIDTH
COL_GT = COL_VW + KV_WIDTH

LANES = 128
VMEM_LIMIT = 56 * 1024 * 1024
ROW_TILE = 512
COL_TILE = 512


def _params(sem, vmem=VMEM_LIMIT):
    return pltpu.CompilerParams(dimension_semantics=sem, vmem_limit_bytes=vmem)


def _rms_scale(x, gain_row):
    ms = jnp.mean(x * x, axis=-1, keepdims=True)
    return x * lax.rsqrt(ms + EPS) * gain_row


def _masked_softmax(s, mask):
    s = jnp.where(mask, s, NEG)
    m = jnp.max(s, axis=-1, keepdims=True)
    p = jnp.exp(s - m) * mask.astype(F32)
    denom = jnp.maximum(jnp.sum(p, axis=-1, keepdims=True), 1e-30)
    return p * (1.0 / denom)


def _dot_nt(a, b):
    return lax.dot_general(a, b, (((1,), (1,)), ((), ())), preferred_element_type=F32)


def _norm_matmul_kernel(x_ref, g_ref, w_ref, ws_ref, o_ref, os_ref, xn_ref):
    @pl.when(pl.program_id(1) == 0)
    def _():
        xn_ref[...] = _rms_scale(x_ref[...], g_ref[...]).astype(BF16)
        os_ref[...] = jnp.dot(xn_ref[...], ws_ref[...], preferred_element_type=F32)

    o_ref[...] = jnp.dot(xn_ref[...], w_ref[...], preferred_element_type=F32).astype(o_ref.dtype)


def _norm_matmul(x, gain, w, w_side, tm, tn):
    m, d = x.shape
    n = w.shape[1]
    ns = w_side.shape[1]
    return pl.pallas_call(
        _norm_matmul_kernel,
        grid=(m // tm, n // tn),
        in_specs=[
            pl.BlockSpec((tm, d), lambda i, j: (i, 0)),
            pl.BlockSpec((1, d), lambda i, j: (0, 0)),
            pl.BlockSpec((d, tn), lambda i, j: (0, j)),
            pl.BlockSpec((d, ns), lambda i, j: (0, 0)),
        ],
        out_specs=[
            pl.BlockSpec((tm, tn), lambda i, j: (i, j)),
            pl.BlockSpec((tm, ns), lambda i, j: (i, 0)),
        ],
        out_shape=[jax.ShapeDtypeStruct((m, n), BF16), jax.ShapeDtypeStruct((m, ns), F32)],
        scratch_shapes=[pltpu.VMEM((tm, d), BF16)],
        compiler_params=_params(("arbitrary", "arbitrary")),
        name="norm_matmul",
    )(x, gain.reshape(1, d), w, w_side)


def _sgu_kernel(u_ref, v_ref, w_ref, bt_ref, gain_ref, o_ref, *, chunks):
    row = lax.broadcasted_iota(jnp.int32, (SGU_CHUNK, SGU_CHUNK), 0)
    col = lax.broadcasted_iota(jnp.int32, (SGU_CHUNK, SGU_CHUNK), 1)
    tri = row >= col
    for g in range(SGU_GROUPS):
        w = jnp.where(tri, w_ref[g], 0.0).astype(BF16)
        bias = bt_ref[:, g:g + 1]
        gain = gain_ref[g:g + 1, :]
        cs = slice(g * HEAD_DIM, (g + 1) * HEAD_DIM)
        for c in range(chunks):
            rs = slice(c * SGU_CHUNK, (c + 1) * SGU_CHUNK)
            u = jax.nn.gelu(u_ref[rs, cs].astype(F32))
            v = _rms_scale(jax.nn.gelu(v_ref[rs, cs].astype(F32)), gain)
            sg = jnp.dot(w, v.astype(BF16), preferred_element_type=F32) + bias
            o_ref[rs, cs] = (u * sg).astype(o_ref.dtype)


def _sgu(proj, sgu_w, sgu_b, sgu_g, tr):
    m = proj.shape[0]
    chunks = tr // SGU_CHUNK
    return pl.pallas_call(
        functools.partial(_sgu_kernel, chunks=chunks),
        grid=(m // tr,),
        in_specs=[
            pl.BlockSpec((tr, SGU_WIDTH), lambda i: (i, COL_U // SGU_WIDTH)),
            pl.BlockSpec((tr, SGU_WIDTH), lambda i: (i, COL_V // SGU_WIDTH)),
            pl.BlockSpec((SGU_GROUPS, SGU_CHUNK, SGU_CHUNK), lambda i: (0, 0, 0)),
            pl.BlockSpec((SGU_CHUNK, SGU_GROUPS), lambda i: (0, 0)),
            pl.BlockSpec((SGU_GROUPS, HEAD_DIM), lambda i: (0, 0)),
        ],
        out_specs=pl.BlockSpec((tr, SGU_WIDTH), lambda i: (i, 0)),
        out_shape=jax.ShapeDtypeStruct((m, SGU_WIDTH), BF16),
        compiler_params=_params(("arbitrary",)),
        name="sgu_mixer",
    )(proj, proj, sgu_w, sgu_b.T, sgu_g)


def _compress_kernel(x_ref, pe_ref, w1_ref, w2_ref, o_ref, xf_ref, *, seq):
    nc = seq // CMP_STRIDE
    xf_ref[0:seq, :] = x_ref[0].astype(F32)
    xf_ref[seq:seq + CMP_STRIDE, :] = jnp.zeros((CMP_STRIDE, HEAD_DIM), F32)
    acc = jnp.zeros((nc, HEAD_DIM), F32)
    for l in range(CMP_LEN):
        rows = xf_ref[pl.ds(l, nc, stride=CMP_STRIDE), :] + pe_ref[0, l:l + 1, :]
        acc = acc + jnp.dot(rows.astype(BF16), w1_ref[0, l], preferred_element_type=F32)
    hid = jax.nn.gelu(acc)
    o_ref[0, 0] = jnp.dot(hid.astype(BF16), w2_ref[0], preferred_element_type=F32).astype(o_ref.dtype)


def _compress(proj3, pe, w1, w2):
    b, s, _ = proj3.shape
    nc = s // CMP_STRIDE
    g = NSA_KV_GROUPS
    col0 = COL_KC // HEAD_DIM
    return pl.pallas_call(
        functools.partial(_compress_kernel, seq=s),
        grid=(2, b, g),
        in_specs=[
            pl.BlockSpec((1, s, HEAD_DIM), lambda kv, bi, gi: (bi, 0, col0 + g * kv + gi)),
            pl.BlockSpec((1, CMP_LEN, HEAD_DIM), lambda kv, bi, gi: (kv, 0, 0)),
            pl.BlockSpec((1, CMP_LEN, HEAD_DIM, HEAD_DIM), lambda kv, bi, gi: (kv, 0, 0, 0)),
            pl.BlockSpec((1, HEAD_DIM, HEAD_DIM), lambda kv, bi, gi: (kv, 0, 0)),
        ],
        out_specs=pl.BlockSpec((1, 1, nc, HEAD_DIM), lambda kv, bi, gi: (kv, bi * g + gi, 0, 0)),
        out_shape=jax.ShapeDtypeStruct((2, b * g, nc, HEAD_DIM), BF16),
        scratch_shapes=[pltpu.VMEM((s + CMP_STRIDE, HEAD_DIM), F32)],
        compiler_params=_params(("arbitrary", "arbitrary", "arbitrary")),
        name="kv_compress",
    )(proj3, pe, w1, w2)


def _importance_matrix(nc, n_slc):
    r1 = SLC_LEN // CMP_STRIDE
    r2 = CMP_LEN // CMP_STRIDE
    mat = np.zeros((nc, n_slc), np.float32)
    for j in range(n_slc):
        for m_ in range(r1):
            for n_ in range(r2):
                n = r1 * j + m_ - n_
                if 0 <= n < nc:
                    mat[n, j] += 1.0
    return mat


CMP_BAND = 256


def _cmp_select_kernel(slopes_ref, q_ref, kc_ref, vc_ref, m_ref, oc_ref, idx_ref, imp_ref, *, nc, n_slc, top_n):
    g = pl.program_id(1)
    q0 = pl.program_id(2) * Q_BLOCK
    t = q0 + lax.broadcasted_iota(jnp.int32, (Q_BLOCK, 1), 0)
    band_cols = min(CMP_BAND, nc)
    n_bands = nc // band_cols
    band = ((q0 + Q_BLOCK) // CMP_STRIDE - 1) // band_cols

    def attend(cols):
        n_ix = lax.broadcasted_iota(jnp.int32, (1, cols), 1)
        valid = (n_ix * CMP_STRIDE + (CMP_LEN - 1)) <= t
        dist = t.astype(F32) - ((n_ix * CMP_STRIDE).astype(F32) + (CMP_LEN - 1) / 2)
        kc = kc_ref[0, 0, :cols, :]
        vc = vc_ref[0, 0, :cols, :]
        psum = jnp.zeros((Q_BLOCK, cols), F32)
        for h in range(NSA_REP):
            cs = slice(h * HEAD_DIM, (h + 1) * HEAD_DIM)
            slope = slopes_ref[g * NSA_REP + h]
            s = _dot_nt(q_ref[0, :, cs], kc) * ATTN_SCALE - slope * dist
            p = _masked_softmax(s, valid)
            psum = psum + p
            oc_ref[0, :, cs] = jnp.dot(p.astype(BF16), vc, preferred_element_type=F32).astype(oc_ref.dtype)
        hi = psum.astype(BF16)
        lo = (psum - hi.astype(F32)).astype(BF16)
        mat = m_ref[:cols, :]
        imp_ref[...] = jnp.dot(hi, mat, preferred_element_type=F32) + jnp.dot(lo, mat, preferred_element_type=F32)

    for k in range(n_bands):
        pl.when(band == k)(functools.partial(attend, (k + 1) * band_cols))

    imp = imp_ref[...]
    j_ix = lax.broadcasted_iota(jnp.int32, (1, n_slc), 1)
    cur = t // SLC_LEN
    causal = j_ix <= cur
    forced = (j_ix == 0) | (j_ix == cur) | (j_ix == cur - 1)
    score = jnp.where(forced, FORCE_SCORE, jnp.where(causal, imp, -1.0))
    j_f = j_ix.astype(F32)
    r_ix = lax.broadcasted_iota(jnp.int32, (1, top_n), 1)
    idx = jnp.zeros((Q_BLOCK, top_n), jnp.int32)
    for r in range(top_n):
        best = jnp.max(score, axis=-1, keepdims=True)
        pick = jnp.min(jnp.where(score == best, j_f, float(n_slc)), axis=-1, keepdims=True)
        score = jnp.where(j_f == pick, REMOVED, score)
        idx = jnp.where(r_ix == r, pick.astype(jnp.int32), idx)
    idx_ref[0, 0] = idx


def _cmp_select(proj3, cmp_kv, slopes, imp_mat, top_n):
    b, s, _ = proj3.shape
    g = NSA_KV_GROUPS
    nc = s // CMP_STRIDE
    n_slc = s // SLC_LEN
    grid_spec = pltpu.PrefetchScalarGridSpec(
        num_scalar_prefetch=1,
        grid=(b, g, s // Q_BLOCK),
        in_specs=[
            pl.BlockSpec((1, Q_BLOCK, GROUP_Q_WIDTH), lambda bi, gi, qi, sl: (bi, qi, COL_Q // GROUP_Q_WIDTH + gi)),
            pl.BlockSpec((1, 1, nc, HEAD_DIM), lambda bi, gi, qi, sl: (0, bi * g + gi, 0, 0)),
            pl.BlockSpec((1, 1, nc, HEAD_DIM), lambda bi, gi, qi, sl: (1, bi * g + gi, 0, 0)),
            pl.BlockSpec((nc, n_slc), lambda bi, gi, qi, sl: (0, 0)),
        ],
        out_specs=[
            pl.BlockSpec((1, Q_BLOCK, GROUP_Q_WIDTH), lambda bi, gi, qi, sl: (bi, qi, gi)),
            pl.BlockSpec((1, 1, Q_BLOCK, top_n), lambda bi, gi, qi, sl: (bi, gi, qi, 0)),
        ],
        scratch_shapes=[pltpu.VMEM((Q_BLOCK, n_slc), F32)],
    )
    return pl.pallas_call(
        functools.partial(_cmp_select_kernel, nc=nc, n_slc=n_slc, top_n=top_n),
        grid_spec=grid_spec,
        out_shape=[
            jax.ShapeDtypeStruct((b, s, NSA_WIDTH), F32),
            jax.ShapeDtypeStruct((b, g, s, top_n), jnp.int32),
        ],
        compiler_params=_params(("arbitrary", "arbitrary", "arbitrary")),
        name="cmp_attn_select",
    )(slopes, proj3, cmp_kv, cmp_kv, imp_mat)


KT_ROWS = HEAD_DIM + 16
POS_HI = float(LANES)
Q_ROWS = 8
GRP = 4
SLC_Q_TILE = 256


def _slc_kernel(slopes_ref, idx_ref, q_ref, k_ref, v_ref, o_ref, qs_ref, os_ref, kt_ref, s_buf, e_buf, *, top_n, seq, qb):
    g = pl.program_id(1)
    q0 = pl.program_id(2) * qb
    n_tiles = top_n * SLC_LEN // LANES
    n_grp = qb // GRP

    @pl.when(pl.program_id(2) == 0)
    def _():
        sub = lax.broadcasted_iota(jnp.int32, (KT_ROWS - HEAD_DIM, LANES), 0)
        lane_f = lax.broadcasted_iota(jnp.int32, (KT_ROWS - HEAD_DIM, LANES), 1).astype(F32)

        def tr_body(i, carry):
            r0 = pl.multiple_of(i * LANES, LANES)
            tile = k_ref[0, pl.ds(r0, LANES), :].astype(F32).T
            pos_rows = jnp.where(sub == 0, lax.convert_element_type(i, F32), jnp.where(sub == 1, lane_f, 0.0))
            ext = jnp.concatenate([tile, pos_rows], axis=0)
            kt_ref[2 * i] = ext.astype(BF16)
            kt_ref[2 * i + 1] = pltpu.roll(ext, SLC_LEN, 1).astype(BF16)
            return carry
        lax.fori_loop(0, seq // LANES, tr_body, 0)
        s_buf[...] = jnp.zeros_like(s_buf)
        e_buf[...] = jnp.zeros_like(e_buf)
        qs_ref[...] = jnp.zeros_like(qs_ref)

    for h in range(NSA_REP):
        qs_ref[pl.ds(h, qb, stride=Q_ROWS), :] = q_ref[0, :, h * HEAD_DIM:(h + 1) * HEAD_DIM].astype(F32)
    row = lax.broadcasted_iota(jnp.int32, (Q_ROWS, 1), 0)
    low_rows = row < NSA_REP
    e_row = lax.broadcasted_iota(jnp.int32, (Q_ROWS, KT_ROWS - HEAD_DIM), 0)
    e_col = lax.broadcasted_iota(jnp.int32, (Q_ROWS, KT_ROWS - HEAD_DIM), 1)
    slope_rows = []
    pos_cols = []
    pos_row = [NSA_REP, 0]
    for par in range(2):
        sl = jnp.zeros((Q_ROWS, 1), F32)
        for h in range(NSA_REP):
            sl = jnp.where(row == h + par * NSA_REP, slopes_ref[g * NSA_REP + h], sl)
        slope_rows.append(sl)
        pos_cols.append(jnp.where((e_row == pos_row[par]) & (e_col == 0), POS_HI,
                                  jnp.where((e_row == pos_row[par]) & (e_col == 1), 1.0, 0.0)).astype(BF16))
    low_half = lax.broadcasted_iota(jnp.int32, (KT_ROWS, LANES), 1) < SLC_LEN

    def block_ids(tloc):
        return [idx_ref[0, 0, tloc, r] for r in range(top_n)]

    def stage_scores(grp, slot):
        for qi in range(GRP):
            par = qi % 2
            tloc = grp * GRP + qi
            ids = block_ids(tloc)
            q_rows = qs_ref[pl.ds(pl.multiple_of(tloc * Q_ROWS, Q_ROWS), Q_ROWS), :]
            if par:
                q_rows = pltpu.roll(q_rows, NSA_REP, 0)
            lhs = jnp.concatenate([q_rows.astype(BF16), pos_cols[par]], axis=1)
            tiles = [jnp.where(low_half, kt_ref[ids[2 * c]], kt_ref[ids[2 * c + 1] ^ 1]) for c in range(n_tiles)]
            s_buf[slot, qi] = jnp.dot(lhs, jnp.concatenate(tiles, axis=1), preferred_element_type=F32)

    def stage_softmax(grp, slot):
        for qi in range(GRP):
            par = qi % 2
            t_f = lax.convert_element_type(q0 + grp * GRP + qi, F32)
            raw = s_buf[slot, qi]
            key_pos = raw[pos_row[par]:pos_row[par] + 1, :]
            mask = key_pos <= t_f
            s = raw * ATTN_SCALE - slope_rows[par] * (t_f - key_pos)
            e_buf[slot, qi] = _masked_softmax(s, mask)

    def stage_values(grp, slot):
        for w in range(GRP // 2):
            tloc = grp * GRP + 2 * w
            v_pair = []
            for par in range(2):
                v_pair.append(jnp.concatenate([v_ref[0, j] for j in block_ids(tloc + par)], axis=0))
            prob = jnp.where(low_rows, e_buf[slot, 2 * w], e_buf[slot, 2 * w + 1]).astype(BF16)
            acc = jnp.dot(prob, jnp.concatenate(v_pair, axis=1), preferred_element_type=F32)
            os_ref[pl.ds(pl.multiple_of(tloc * NSA_REP, 2 * NSA_REP), 2 * NSA_REP), :] = jnp.where(
                low_rows, acc[:, :HEAD_DIM], acc[:, HEAD_DIM:])

    def body(it, carry):
        cur = it % 2
        stage_values(jnp.clip(it - 2, 0, n_grp - 1), cur)
        stage_softmax(jnp.clip(it - 1, 0, n_grp - 1), 1 - cur)
        stage_scores(jnp.minimum(it, n_grp - 1), cur)
        return carry

    lax.fori_loop(0, n_grp + 2, body, 0)
    for h in range(NSA_REP):
        o_ref[0, :, h * HEAD_DIM:(h + 1) * HEAD_DIM] = os_ref[pl.ds(h, qb, stride=NSA_REP), :].astype(o_ref.dtype)


def _slc_attention(proj3, idx, slopes, top_n):
    b, s, width = proj3.shape
    g = NSA_KV_GROUPS
    n_slc = s // SLC_LEN
    qb = _pick_tile(s, SLC_Q_TILE)
    assert top_n % (LANES // SLC_LEN) == 0 and qb % GRP == 0 and GRP % 2 == 0
    proj4 = proj3.reshape(b, n_slc, SLC_LEN, width)
    grid_spec = pltpu.PrefetchScalarGridSpec(
        num_scalar_prefetch=1,
        grid=(b, g, s // qb),
        in_specs=[
            pl.BlockSpec((1, 1, qb, top_n), lambda bi, gi, qi, sl: (bi, gi, qi, 0), memory_space=pltpu.SMEM),
            pl.BlockSpec((1, qb, GROUP_Q_WIDTH), lambda bi, gi, qi, sl: (bi, qi, COL_Q // GROUP_Q_WIDTH + gi)),
            pl.BlockSpec((1, s, HEAD_DIM), lambda bi, gi, qi, sl: (bi, 0, COL_KS // HEAD_DIM + gi)),
            pl.BlockSpec((1, n_slc, SLC_LEN, HEAD_DIM), lambda bi, gi, qi, sl: (bi, 0, 0, COL_VS // HEAD_DIM + gi)),
        ],
        out_specs=pl.BlockSpec((1, qb, GROUP_Q_WIDTH), lambda bi, gi, qi, sl: (bi, qi, gi)),
        scratch_shapes=[
            pltpu.VMEM((qb * Q_ROWS, HEAD_DIM), F32),
            pltpu.VMEM((qb * NSA_REP, HEAD_DIM), F32),
            pltpu.VMEM((n_slc, KT_ROWS, LANES), BF16),
            pltpu.VMEM((2, GRP, Q_ROWS, top_n * SLC_LEN), F32),
            pltpu.VMEM((2, GRP, Q_ROWS, top_n * SLC_LEN), F32),
        ],
    )
    return pl.pallas_call(
        functools.partial(_slc_kernel, top_n=top_n, seq=s, qb=qb),
        grid_spec=grid_spec,
        out_shape=jax.ShapeDtypeStruct((b, s, NSA_WIDTH), F32),
        compiler_params=_params(("arbitrary", "arbitrary", "arbitrary")),
        name="slc_attn",
    )(slopes, idx, proj3, proj3, proj4)


def _win_kernel(slopes_ref, q_ref, k_ref, v_ref, oc_ref, os_ref, gt_ref, y_ref, *, span):
    g = pl.program_id(1)
    q0 = pl.program_id(2) * Q_BLOCK
    start = pl.multiple_of(jnp.maximum(q0 - WINDOW, 0), Q_BLOCK)
    kw = k_ref[0, pl.ds(start, span), :]
    vw = v_ref[0, pl.ds(start, span), :]
    t = q0 + lax.broadcasted_iota(jnp.int32, (Q_BLOCK, 1), 0)
    s_pos = start + lax.broadcasted_iota(jnp.int32, (1, span), 1)
    rel = t - s_pos
    mask = (rel >= 0) & (rel < WINDOW)
    rel_f = rel.astype(F32)
    gates = jax.nn.sigmoid(gt_ref[...])
    for h in range(NSA_REP):
        cs = slice(h * HEAD_DIM, (h + 1) * HEAD_DIM)
        slope = slopes_ref[g * NSA_REP + h]
        s = _dot_nt(q_ref[0, :, cs], kw) * ATTN_SCALE - slope * rel_f
        p = _masked_softmax(s, mask)
        ow = jnp.dot(p.astype(BF16), vw, preferred_element_type=F32)
        c = h * N_BRANCH
        y = gates[:, c:c + 1] * oc_ref[0, :, cs] + gates[:, c + 1:c + 2] * os_ref[0, :, cs] + gates[:, c + 2:c + 3] * ow
        y_ref[0, :, cs] = y.astype(y_ref.dtype)


def _win_combine(proj3, o_cmp, o_slc, gates_raw, slopes):
    b, s, _ = proj3.shape
    g = NSA_KV_GROUPS
    nq = s // Q_BLOCK
    span = min(WINDOW + Q_BLOCK, s)
    grid_spec = pltpu.PrefetchScalarGridSpec(
        num_scalar_prefetch=1,
        grid=(b, g, nq),
        in_specs=[
            pl.BlockSpec((1, Q_BLOCK, GROUP_Q_WIDTH), lambda bi, gi, qi, sl: (bi, qi, COL_Q // GROUP_Q_WIDTH + gi)),
            pl.BlockSpec((1, s, HEAD_DIM), lambda bi, gi, qi, sl: (bi, 0, COL_KW // HEAD_DIM + gi)),
            pl.BlockSpec((1, s, HEAD_DIM), lambda bi, gi, qi, sl: (bi, 0, COL_VW // HEAD_DIM + gi)),
            pl.BlockSpec((1, Q_BLOCK, GROUP_Q_WIDTH), lambda bi, gi, qi, sl: (bi, qi, gi)),
            pl.BlockSpec((1, Q_BLOCK, GROUP_Q_WIDTH), lambda bi, gi, qi, sl: (bi, qi, gi)),
            pl.BlockSpec((Q_BLOCK, LANES), lambda bi, gi, qi, sl: (bi * nq + qi, gi)),
        ],
        out_specs=pl.BlockSpec((1, Q_BLOCK, GROUP_Q_WIDTH), lambda bi, gi, qi, sl: (bi, qi, gi)),
    )
    return pl.pallas_call(
        functools.partial(_win_kernel, span=span),
        grid_spec=grid_spec,
        out_shape=jax.ShapeDtypeStruct((b, s, NSA_WIDTH), BF16),
        compiler_params=_params(("arbitrary", "arbitrary", "arbitrary")),
        name="win_attn_combine",
    )(slopes, proj3, proj3, proj3, o_cmp, o_slc, gates_raw)


def _matmul_res_kernel(*refs, n_pairs):
    r_ref = refs[2 * n_pairs]
    o_ref = refs[2 * n_pairs + 1]
    acc = r_ref[...]
    for i in range(n_pairs):
        acc = acc + jnp.dot(refs[i][...], refs[n_pairs + i][...], preferred_element_type=F32)
    o_ref[...] = acc


def _matmul_res(acts, weights, resid, tm, tn):
    m, n = resid.shape
    n_pairs = len(acts)
    in_specs = [pl.BlockSpec((tm, a.shape[1]), lambda i, j: (i, 0)) for a in acts]
    in_specs += [pl.BlockSpec((w.shape[0], tn), lambda i, j: (0, j)) for w in weights]
    in_specs += [pl.BlockSpec((tm, tn), lambda i, j: (i, j))]
    return pl.pallas_call(
        functools.partial(_matmul_res_kernel, n_pairs=n_pairs),
        grid=(m // tm, n // tn),
        in_specs=in_specs,
        out_specs=pl.BlockSpec((tm, tn), lambda i, j: (i, j)),
        out_shape=jax.ShapeDtypeStruct((m, n), F32),
        compiler_params=_params(("arbitrary", "arbitrary")),
        name="matmul_residual",
    )(*acts, *weights, resid)


def _ffn_kernel(h_ref, g_ref, wg_ref, wu_ref, wd_ref, gf_ref, o_ref, xn_ref, acc_ref, *, final_norm):
    k = pl.program_id(1)

    @pl.when(k == 0)
    def _():
        xn_ref[...] = _rms_scale(h_ref[...], g_ref[...]).astype(BF16)
        acc_ref[...] = jnp.zeros_like(acc_ref)

    xn = xn_ref[...]
    a = jnp.dot(xn, wg_ref[...], preferred_element_type=F32)
    u = jnp.dot(xn, wu_ref[...], preferred_element_type=F32)
    act = (a * jax.nn.sigmoid(a) * u).astype(BF16)
    acc_ref[...] += jnp.dot(act, wd_ref[...], preferred_element_type=F32)

    @pl.when(k == pl.num_programs(1) - 1)
    def _():
        out = h_ref[...] + acc_ref[...]
        if final_norm:
            out = _rms_scale(out, gf_ref[...])
        o_ref[...] = out


def _ffn(h, gain, w_gate, w_up, w_down, gain_final, final_norm, tm, tk):
    m, d = h.shape
    dff = w_gate.shape[1]
    return pl.pallas_call(
        functools.partial(_ffn_kernel, final_norm=final_norm),
        grid=(m // tm, dff // tk),
        in_specs=[
            pl.BlockSpec((tm, d), lambda i, k: (i, 0)),
            pl.BlockSpec((1, d), lambda i, k: (0, 0)),
            pl.BlockSpec((d, tk), lambda i, k: (0, k)),
            pl.BlockSpec((d, tk), lambda i, k: (0, k)),
            pl.BlockSpec((tk, d), lambda i, k: (k, 0)),
            pl.BlockSpec((1, d), lambda i, k: (0, 0)),
        ],
        out_specs=pl.BlockSpec((tm, d), lambda i, k: (i, 0)),
        out_shape=jax.ShapeDtypeStruct((m, d), F32),
        scratch_shapes=[pltpu.VMEM((tm, d), BF16), pltpu.VMEM((tm, d), F32)],
        compiler_params=_params(("arbitrary", "arbitrary")),
        name="ffn_swiglu",
    )(h, gain.reshape(1, d), w_gate, w_up, w_down, gain_final.reshape(1, d))


TAIL_ROWS = 8


def _conv_in_kernel(x_ref, g_ref, wb_ref, wc_ref, wz_ref, cw_ref, y_ref, xn_ref, tail_ref, *, tiles_per_seq):
    i = pl.program_id(0)
    j = pl.program_id(1)

    @pl.when(j == 0)
    def _():
        xn_ref[...] = _rms_scale(x_ref[...], g_ref[...]).astype(BF16)

    xn = xn_ref[...]
    bg = jnp.dot(xn, wb_ref[...], preferred_element_type=F32)
    z = jnp.dot(xn, wc_ref[...], preferred_element_type=F32) * jnp.dot(xn, wz_ref[...], preferred_element_type=F32)
    tm = z.shape[0]
    seq_start = (i % tiles_per_seq) == 0
    prev = jnp.where(seq_start, 0.0, tail_ref[j])
    tail_ref[j] = z[tm - TAIL_ROWS:, :]
    prev1 = prev[TAIL_ROWS - 1:TAIL_ROWS, :]
    prev2 = prev[TAIL_ROWS - 2:TAIL_ROWS - 1, :]
    row = lax.broadcasted_iota(jnp.int32, (tm, 1), 0)
    z1 = jnp.where(row >= 1, pltpu.roll(z, 1, 0), prev1)
    z2 = jnp.where(row >= 2, pltpu.roll(z, 2, 0), jnp.where(row == 1, prev1, prev2))
    zc = cw_ref[2:3, :] * z + cw_ref[1:2, :] * z1 + cw_ref[0:1, :] * z2
    y_ref[...] = (bg * zc).astype(y_ref.dtype)


def _conv_in(h, gain, w_in, conv_w, seq, tm, tn):
    m, d = h.shape
    cw = conv_w.shape[1]
    nj = cw // tn
    return pl.pallas_call(
        functools.partial(_conv_in_kernel, tiles_per_seq=seq // tm),
        grid=(m // tm, nj),
        in_specs=[
            pl.BlockSpec((tm, d), lambda i, j: (i, 0)),
            pl.BlockSpec((1, d), lambda i, j: (0, 0)),
            pl.BlockSpec((d, tn), lambda i, j: (0, j)),
            pl.BlockSpec((d, tn), lambda i, j: (0, nj + j)),
            pl.BlockSpec((d, tn), lambda i, j: (0, 2 * nj + j)),
            pl.BlockSpec((conv_w.shape[0], tn), lambda i, j: (0, j)),
        ],
        out_specs=pl.BlockSpec((tm, tn), lambda i, j: (i, j)),
        out_shape=jax.ShapeDtypeStruct((m, cw), BF16),
        scratch_shapes=[pltpu.VMEM((tm, d), BF16), pltpu.VMEM((nj, TAIL_ROWS, tn), F32)],
        compiler_params=_params(("arbitrary", "arbitrary")),
        name="conv_mixer_in",
    )(h, gain.reshape(1, d), w_in, w_in, w_in, conv_w)


def _gate_weight(w_in):
    per_group = NSA_REP * N_BRANCH
    cols = []
    for g in range(NSA_KV_GROUPS):
        blk = w_in[:, COL_GT + g * per_group:COL_GT + (g + 1) * per_group]
        cols.append(jnp.pad(blk, ((0, 0), (0, LANES - per_group))))
    return jnp.concatenate(cols, axis=1)


def _pick_tile(m, pref):
    t = min(m, pref)
    while m % t:
        t //= 2
    return t


def kernel(x, norm_mix, norm_ffn, norm_f, w_in_ab, w_out_ab, sgu_w, sgu_b, sgu_g, cmp_pe_k, cmp_w1_k, cmp_w2_k, cmp_pe_v, cmp_w1_v, cmp_w2_v, w_in_c, conv_w, w_out_c, w_gate, w_up, w_down):
    b, s, d = x.shape
    m = b * s
    tm = _pick_tile(m, ROW_TILE)
    tm_wide = _pick_tile(s, 2 * ROW_TILE)
    top_n = min(SLC_TOP_N, s // SLC_LEN)
    slopes = (2.0 ** (-8.0 * jnp.arange(1, NSA_HEADS + 1, dtype=F32) / NSA_HEADS)).astype(F32)
    imp_mat = jnp.asarray(_importance_matrix(s // CMP_STRIDE, s // SLC_LEN), BF16)

    h = x.reshape(m, d)

    w_in = w_in_ab[0]
    proj, gates_raw = _norm_matmul(h, norm_mix[0], w_in[:, :COL_GT].astype(BF16), _gate_weight(w_in).astype(BF16),
                                   tm_wide, COL_GT // 3)
    y_a = _sgu(proj, sgu_w[0], sgu_b[0], sgu_g[0], tm)
    proj3 = proj.reshape(b, s, COL_GT)
    cmp_kv = _compress(
        proj3,
        jnp.stack([cmp_pe_k[0], cmp_pe_v[0]]),
        jnp.stack([cmp_w1_k[0], cmp_w1_v[0]]).astype(BF16),
        jnp.stack([cmp_w2_k[0], cmp_w2_v[0]]).astype(BF16),
    )
    o_cmp, idx = _cmp_select(proj3, cmp_kv, slopes, imp_mat, top_n)
    o_slc = _slc_attention(proj3, idx, slopes, top_n)
    y_b = _win_combine(proj3, o_cmp, o_slc, gates_raw, slopes)
    w_out = w_out_ab[0].astype(BF16)
    h = _matmul_res([y_a, y_b.reshape(m, NSA_WIDTH)], [w_out[:SGU_WIDTH], w_out[SGU_WIDTH:]], h, tm, d)
    h = _ffn(h, norm_ffn[0], w_gate[0].astype(BF16), w_up[0].astype(BF16), w_down[0].astype(BF16), norm_f, False, tm, COL_TILE)

    y_c = _conv_in(h, norm_mix[1], w_in_c[0].astype(BF16), conv_w[0], s, tm_wide, COL_TILE)
    h = _matmul_res([y_c], [w_out_c[0].astype(BF16)], h, tm, d)
    h = _ffn(h, norm_ffn[1], w_gate[1].astype(BF16), w_up[1].astype(BF16), w_down[1].astype(BF16), norm_f, True, tm, COL_TILE)
    return h.reshape(b, s, d)
```

```python
import functools
import math

import jax
import jax.numpy as jnp
import numpy as np
from jax import lax
from jax.experimental import pallas as pl
from jax.experimental.pallas import tpu as pltpu

F32 = jnp.float32
BF16 = jnp.bfloat16

HEAD_DIM = 128
SGU_GROUPS = 8
SGU_CHUNK = 128
SGU_WIDTH = SGU_GROUPS * HEAD_DIM
NSA_HEADS = 8
NSA_KV_GROUPS = 2
NSA_REP = NSA_HEADS // NSA_KV_GROUPS
NSA_WIDTH = NSA_HEADS * HEAD_DIM
KV_WIDTH = NSA_KV_GROUPS * HEAD_DIM
GROUP_Q_WIDTH = NSA_REP * HEAD_DIM
CMP_LEN = 32
CMP_STRIDE = 16
SLC_LEN = 64
SLC_TOP_N = 16
WINDOW = 512
Q_BLOCK = 128
N_BRANCH = 3
FORCE_SCORE = 1e9
EPS = 1e-6
NEG = -1e30
REMOVED = -3e38
ATTN_SCALE = HEAD_DIM ** -0.5

COL_U = 0
COL_V = SGU_WIDTH
COL_Q = 2 * SGU_WIDTH
COL_KC = COL_Q + NSA_WIDTH
COL_VC = COL_KC + KV_WIDTH
COL_KS = COL_VC + KV_WIDTH
COL_VS = COL_KS + KV_WIDTH
COL_KW = COL_VS + KV_WIDTH
COL_VW = COL_KW + KV_WIDTH
COL_GT = COL_VW + KV_WIDTH

LANES = 128
VMEM_LIMIT = 56 * 1024 * 1024
ROW_TILE = 512
COL_TILE = 512


def _params(sem, vmem=VMEM_LIMIT):
    return pltpu.CompilerParams(dimension_semantics=sem, vmem_limit_bytes=vmem)


def _rms_scale(x, gain_row):
    ms = jnp.mean(x * x, axis=-1, keepdims=True)
    return x * lax.rsqrt(ms + EPS) * gain_row


def _masked_softmax(s, mask):
    s = jnp.where(mask, s, NEG)
    m = jnp.max(s, axis=-1, keepdims=True)
    p = jnp.exp(s - m) * mask.astype(F32)
    denom = jnp.maximum(jnp.sum(p, axis=-1, keepdims=True), 1e-30)
    return p * (1.0 / denom)


def _dot_nt(a, b):
    return lax.dot_general(a, b, (((1,), (1,)), ((), ())), preferred_element_type=F32)


def _norm_matmul_kernel(x_ref, g_ref, w_ref, ws_ref, o_ref, os_ref, xn_ref):
    @pl.when(pl.program_id(1) == 0)
    def _():
        xn_ref[...] = _rms_scale(x_ref[...], g_ref[...]).astype(BF16)
        os_ref[...] = jnp.dot(xn_ref[...], ws_ref[...], preferred_element_type=F32)

    o_ref[...] = jnp.dot(xn_ref[...], w_ref[...], preferred_element_type=F32).astype(o_ref.dtype)


def _norm_matmul(x, gain, w, w_side, tm, tn):
    m, d = x.shape
    n = w.shape[1]
    ns = w_side.shape[1]
    return pl.pallas_call(
        _norm_matmul_kernel,
        grid=(m // tm, n // tn),
        in_specs=[
            pl.BlockSpec((tm, d), lambda i, j: (i, 0)),
            pl.BlockSpec((1, d), lambda i, j: (0, 0)),
            pl.BlockSpec((d, tn), lambda i, j: (0, j)),
            pl.BlockSpec((d, ns), lambda i, j: (0, 0)),
        ],
        out_specs=[
            pl.BlockSpec((tm, tn), lambda i, j: (i, j)),
            pl.BlockSpec((tm, ns), lambda i, j: (i, 0)),
        ],
        out_shape=[jax.ShapeDtypeStruct((m, n), BF16), jax.ShapeDtypeStruct((m, ns), F32)],
        scratch_shapes=[pltpu.VMEM((tm, d), BF16)],
        compiler_params=_params(("arbitrary", "arbitrary")),
        name="norm_matmul",
    )(x, gain.reshape(1, d), w, w_side)


def _sgu_kernel(u_ref, v_ref, w_ref, bt_ref, gain_ref, o_ref, *, chunks):
    row = lax.broadcasted_iota(jnp.int32, (SGU_CHUNK, SGU_CHUNK), 0)
    col = lax.broadcasted_iota(jnp.int32, (SGU_CHUNK, SGU_CHUNK), 1)
    tri = row >= col
    for g in range(SGU_GROUPS):
        w = jnp.where(tri, w_ref[g], 0.0).astype(BF16)
        bias = bt_ref[:, g:g + 1]
        gain = gain_ref[g:g + 1, :]
        cs = slice(g * HEAD_DIM, (g + 1) * HEAD_DIM)
        for c in range(chunks):
            rs = slice(c * SGU_CHUNK, (c + 1) * SGU_CHUNK)
            u = jax.nn.gelu(u_ref[rs, cs].astype(F32))
            v = _rms_scale(jax.nn.gelu(v_ref[rs, cs].astype(F32)), gain)
            sg = jnp.dot(w, v.astype(BF16), preferred_element_type=F32) + bias
            o_ref[rs, cs] = (u * sg).astype(o_ref.dtype)


def _sgu(proj, sgu_w, sgu_b, sgu_g, tr):
    m = proj.shape[0]
    chunks = tr // SGU_CHUNK
    return pl.pallas_call(
        functools.partial(_sgu_kernel, chunks=chunks),
        grid=(m // tr,),
        in_specs=[
            pl.BlockSpec((tr, SGU_WIDTH), lambda i: (i, COL_U // SGU_WIDTH)),
            pl.BlockSpec((tr, SGU_WIDTH), lambda i: (i, COL_V // SGU_WIDTH)),
            pl.BlockSpec((SGU_GROUPS, SGU_CHUNK, SGU_CHUNK), lambda i: (0, 0, 0)),
            pl.BlockSpec((SGU_CHUNK, SGU_GROUPS), lambda i: (0, 0)),
            pl.BlockSpec((SGU_GROUPS, HEAD_DIM), lambda i: (0, 0)),
        ],
        out_specs=pl.BlockSpec((tr, SGU_WIDTH), lambda i: (i, 0)),
        out_shape=jax.ShapeDtypeStruct((m, SGU_WIDTH), BF16),
        compiler_params=_params(("arbitrary",)),
        name="sgu_mixer",
    )(proj, proj, sgu_w, sgu_b.T, sgu_g)


def _compress_kernel(x_ref, pe_ref, w1_ref, w2_ref, o_ref, xf_ref, *, seq):
    nc = seq // CMP_STRIDE
    xf_ref[0:seq, :] = x_ref[0].astype(F32)
    xf_ref[seq:seq + CMP_STRIDE, :] = jnp.zeros((CMP_STRIDE, HEAD_DIM), F32)
    acc = jnp.zeros((nc, HEAD_DIM), F32)
    for l in range(CMP_LEN):
        rows = xf_ref[pl.ds(l, nc, stride=CMP_STRIDE), :] + pe_ref[0, l:l + 1, :]
        acc = acc + jnp.dot(rows.astype(BF16), w1_ref[0, l], preferred_element_type=F32)
    hid = jax.nn.gelu(acc)
    o_ref[0, 0] = jnp.dot(hid.astype(BF16), w2_ref[0], preferred_element_type=F32).astype(o_ref.dtype)


def _compress(proj3, pe, w1, w2):
    b, s, _ = proj3.shape
    nc = s // CMP_STRIDE
    g = NSA_KV_GROUPS
    col0 = COL_KC // HEAD_DIM
    return pl.pallas_call(
        functools.partial(_compress_kernel, seq=s),
        grid=(2, b, g),
        in_specs=[
            pl.BlockSpec((1, s, HEAD_DIM), lambda kv, bi, gi: (bi, 0, col0 + g * kv + gi)),
            pl.BlockSpec((1, CMP_LEN, HEAD_DIM), lambda kv, bi, gi: (kv, 0, 0)),
            pl.BlockSpec((1, CMP_LEN, HEAD_DIM, HEAD_DIM), lambda kv, bi, gi: (kv, 0, 0, 0)),
            pl.BlockSpec((1, HEAD_DIM, HEAD_DIM), lambda kv, bi, gi: (kv, 0, 0)),
        ],
        out_specs=pl.BlockSpec((1, 1, nc, HEAD_DIM), lambda kv, bi, gi: (kv, bi * g + gi, 0, 0)),
        out_shape=jax.ShapeDtypeStruct((2, b * g, nc, HEAD_DIM), BF16),
        scratch_shapes=[pltpu.VMEM((s + CMP_STRIDE, HEAD_DIM), F32)],
        compiler_params=_params(("arbitrary", "arbitrary", "arbitrary")),
        name="kv_compress",
    )(proj3, pe, w1, w2)


def _importance_matrix(nc, n_slc):
    r1 = SLC_LEN // CMP_STRIDE
    r2 = CMP_LEN // CMP_STRIDE
    mat = np.zeros((nc, n_slc), np.float32)
    for j in range(n_slc):
        for m_ in range(r1):
            for n_ in range(r2):
                n = r1 * j + m_ - n_
                if 0 <= n < nc:
                    mat[n, j] += 1.0
    return mat


CMP_BAND = 256


def _cmp_select_kernel(slopes_ref, q_ref, kc_ref, vc_ref, m_ref, oc_ref, idx_ref, imp_ref, *, nc, n_slc, top_n):
    g = pl.program_id(1)
    q0 = pl.program_id(2) * Q_BLOCK
    t = q0 + lax.broadcasted_iota(jnp.int32, (Q_BLOCK, 1), 0)
    band_cols = min(CMP_BAND, nc)
    n_bands = nc // band_cols
    band = ((q0 + Q_BLOCK) // CMP_STRIDE - 1) // band_cols

    def attend(cols):
        n_ix = lax.broadcasted_iota(jnp.int32, (1, cols), 1)
        valid = (n_ix * CMP_STRIDE + (CMP_LEN - 1)) <= t
        dist = t.astype(F32) - ((n_ix * CMP_STRIDE).astype(F32) + (CMP_LEN - 1) / 2)
        kc = kc_ref[0, 0, :cols, :]
        vc = vc_ref[0, 0, :cols, :]
        psum = jnp.zeros((Q_BLOCK, cols), F32)
        for h in range(NSA_REP):
            cs = slice(h * HEAD_DIM, (h + 1) * HEAD_DIM)
            slope = slopes_ref[g * NSA_REP + h]
            s = _dot_nt(q_ref[0, :, cs], kc) * ATTN_SCALE - slope * dist
            p = _masked_softmax(s, valid)
            psum = psum + p
            oc_ref[0, :, cs] = jnp.dot(p.astype(BF16), vc, preferred_element_type=F32).astype(oc_ref.dtype)
        hi = psum.astype(BF16)
        lo = (psum - hi.astype(F32)).astype(BF16)
        mat = m_ref[:cols, :]
        imp_ref[...] = jnp.dot(hi, mat, preferred_element_type=F32) + jnp.dot(lo, mat, preferred_element_type=F32)

    for k in range(n_bands):
        pl.when(band == k)(functools.partial(attend, (k + 1) * band_cols))

    imp_t = imp_ref[...].T
    t_row = q0 + lax.broadcasted_iota(jnp.int32, (1, Q_BLOCK), 1)
    j_ix = lax.broadcasted_iota(jnp.int32, (n_slc, 1), 0)
    cur = t_row // SLC_LEN
    causal = j_ix <= cur
    forced = (j_ix == 0) | (j_ix == cur) | (j_ix == cur - 1)
    score = jnp.where(forced, FORCE_SCORE, jnp.where(causal, imp_t, -1.0))
    j_f = j_ix.astype(F32)
    r_ix = lax.broadcasted_iota(jnp.int32, (top_n, 1), 0)
    idx = jnp.zeros((top_n, Q_BLOCK), jnp.int32)
    for r in range(top_n):
        best = jnp.max(score, axis=0, keepdims=True)
        pick = jnp.min(jnp.where(score == best, j_f, float(n_slc)), axis=0, keepdims=True)
        score = jnp.where(j_f == pick, REMOVED, score)
        idx = jnp.where(r_ix == r, pick.astype(jnp.int32), idx)
    idx_ref[0, 0] = idx


def _cmp_select(proj3, cmp_kv, slopes, imp_mat, top_n):
    b, s, _ = proj3.shape
    g = NSA_KV_GROUPS
    nc = s // CMP_STRIDE
    n_slc = s // SLC_LEN
    grid_spec = pltpu.PrefetchScalarGridSpec(
        num_scalar_prefetch=1,
        grid=(b, g, s // Q_BLOCK),
        in_specs=[
            pl.BlockSpec((1, Q_BLOCK, GROUP_Q_WIDTH), lambda bi, gi, qi, sl: (bi, qi, COL_Q // GROUP_Q_WIDTH + gi)),
            pl.BlockSpec((1, 1, nc, HEAD_DIM), lambda bi, gi, qi, sl: (0, bi * g + gi, 0, 0)),
            pl.BlockSpec((1, 1, nc, HEAD_DIM), lambda bi, gi, qi, sl: (1, bi * g + gi, 0, 0)),
            pl.BlockSpec((nc, n_slc), lambda bi, gi, qi, sl: (0, 0)),
        ],
        out_specs=[
            pl.BlockSpec((1, Q_BLOCK, GROUP_Q_WIDTH), lambda bi, gi, qi, sl: (bi, qi, gi)),
            pl.BlockSpec((1, 1, top_n, Q_BLOCK), lambda bi, gi, qi, sl: (bi, gi, 0, qi)),
        ],
        scratch_shapes=[pltpu.VMEM((Q_BLOCK, n_slc), F32)],
    )
    return pl.pallas_call(
        functools.partial(_cmp_select_kernel, nc=nc, n_slc=n_slc, top_n=top_n),
        grid_spec=grid_spec,
        out_shape=[
            jax.ShapeDtypeStruct((b, s, NSA_WIDTH), F32),
            jax.ShapeDtypeStruct((b, g, top_n, s), jnp.int32),
        ],
        compiler_params=_params(("arbitrary", "arbitrary", "arbitrary")),
        name="cmp_attn_select",
    )(slopes, proj3, cmp_kv, cmp_kv, imp_mat)


KT_ROWS = HEAD_DIM + 16
POS_HI = float(LANES)
Q_ROWS = 8
GRP = 4
SLC_Q_TILE = 256


def _slc_kernel(slopes_ref, idx_ref, q_ref, k_ref, v_ref, o_ref, qs_ref, os_ref, kt_ref, s_buf, e_buf, *, top_n, seq, qb):
    g = pl.program_id(1)
    q0 = pl.program_id(2) * qb
    n_tiles = top_n * SLC_LEN // LANES
    n_grp = qb // GRP

    @pl.when(pl.program_id(2) == 0)
    def _():
        sub = lax.broadcasted_iota(jnp.int32, (KT_ROWS - HEAD_DIM, LANES), 0)
        lane_f = lax.broadcasted_iota(jnp.int32, (KT_ROWS - HEAD_DIM, LANES), 1).astype(F32)

        def tr_body(i, carry):
            r0 = pl.multiple_of(i * LANES, LANES)
            tile = k_ref[0, pl.ds(r0, LANES), :].astype(F32).T
            pos_rows = jnp.where(sub == 0, lax.convert_element_type(i, F32), jnp.where(sub == 1, lane_f, 0.0))
            ext = jnp.concatenate([tile, pos_rows], axis=0)
            kt_ref[2 * i] = ext.astype(BF16)
            kt_ref[2 * i + 1] = pltpu.roll(ext, SLC_LEN, 1).astype(BF16)
            return carry
        lax.fori_loop(0, seq // LANES, tr_body, 0)
        s_buf[...] = jnp.zeros_like(s_buf)
        e_buf[...] = jnp.zeros_like(e_buf)
        qs_ref[...] = jnp.zeros_like(qs_ref)

    for h in range(NSA_REP):
        qs_ref[pl.ds(h, qb, stride=Q_ROWS), :] = q_ref[0, :, h * HEAD_DIM:(h + 1) * HEAD_DIM].astype(F32)
    row = lax.broadcasted_iota(jnp.int32, (Q_ROWS, 1), 0)
    low_rows = row < NSA_REP
    e_row = lax.broadcasted_iota(jnp.int32, (Q_ROWS, KT_ROWS - HEAD_DIM), 0)
    e_col = lax.broadcasted_iota(jnp.int32, (Q_ROWS, KT_ROWS - HEAD_DIM), 1)
    slope_rows = []
    pos_cols = []
    pos_row = [NSA_REP, 0]
    for par in range(2):
        sl = jnp.zeros((Q_ROWS, 1), F32)
        for h in range(NSA_REP):
            sl = jnp.where(row == h + par * NSA_REP, slopes_ref[g * NSA_REP + h], sl)
        slope_rows.append(sl)
        pos_cols.append(jnp.where((e_row == pos_row[par]) & (e_col == 0), POS_HI,
                                  jnp.where((e_row == pos_row[par]) & (e_col == 1), 1.0, 0.0)).astype(BF16))
    low_half = lax.broadcasted_iota(jnp.int32, (KT_ROWS, LANES), 1) < SLC_LEN

    def block_ids(tloc):
        return [idx_ref[0, 0, r, tloc] for r in range(top_n)]

    def stage_scores(grp, slot):
        for qi in range(GRP):
            par = qi % 2
            tloc = grp * GRP + qi
            ids = block_ids(tloc)
            q_rows = qs_ref[pl.ds(pl.multiple_of(tloc * Q_ROWS, Q_ROWS), Q_ROWS), :]
            if par:
                q_rows = pltpu.roll(q_rows, NSA_REP, 0)
            lhs = jnp.concatenate([q_rows.astype(BF16), pos_cols[par]], axis=1)
            tiles = [jnp.where(low_half, kt_ref[ids[2 * c]], kt_ref[ids[2 * c + 1] ^ 1]) for c in range(n_tiles)]
            s_buf[slot, qi] = jnp.dot(lhs, jnp.concatenate(tiles, axis=1), preferred_element_type=F32)

    def stage_softmax(grp, slot):
        for qi in range(GRP):
            par = qi % 2
            t_f = lax.convert_element_type(q0 + grp * GRP + qi, F32)
            raw = s_buf[slot, qi]
            key_pos = raw[pos_row[par]:pos_row[par] + 1, :]
            mask = key_pos <= t_f
            s = raw * ATTN_SCALE - slope_rows[par] * (t_f - key_pos)
            e_buf[slot, qi] = _masked_softmax(s, mask)

    def stage_values(grp, slot):
        for w in range(GRP // 2):
            tloc = grp * GRP + 2 * w
            v_pair = []
            for par in range(2):
                v_pair.append(jnp.concatenate([v_ref[0, j] for j in block_ids(tloc + par)], axis=0))
            prob = jnp.where(low_rows, e_buf[slot, 2 * w], e_buf[slot, 2 * w + 1]).astype(BF16)
            acc = jnp.dot(prob, jnp.concatenate(v_pair, axis=1), preferred_element_type=F32)
            os_ref[pl.ds(pl.multiple_of(tloc * NSA_REP, 2 * NSA_REP), 2 * NSA_REP), :] = jnp.where(
                low_rows, acc[:, :HEAD_DIM], acc[:, HEAD_DIM:])

    def body(it, carry):
        cur = it % 2
        stage_values(jnp.clip(it - 2, 0, n_grp - 1), cur)
        stage_softmax(jnp.clip(it - 1, 0, n_grp - 1), 1 - cur)
        stage_scores(jnp.minimum(it, n_grp - 1), cur)
        return carry

    lax.fori_loop(0, n_grp + 2, body, 0)
    for h in range(NSA_REP):
        o_ref[0, :, h * HEAD_DIM:(h + 1) * HEAD_DIM] = os_ref[pl.ds(h, qb, stride=NSA_REP), :].astype(o_ref.dtype)


def _slc_attention(proj3, idx, slopes, top_n):
    b, s, width = proj3.shape
    g = NSA_KV_GROUPS
    n_slc = s // SLC_LEN
    qb = _pick_tile(s, SLC_Q_TILE)
    assert top_n % (LANES // SLC_LEN) == 0 and qb % GRP == 0 and GRP % 2 == 0
    proj4 = proj3.reshape(b, n_slc, SLC_LEN, width)
    grid_spec = pltpu.PrefetchScalarGridSpec(
        num_scalar_prefetch=1,
        grid=(b, g, s // qb),
        in_specs=[
            pl.BlockSpec((1, 1, top_n, qb), lambda bi, gi, qi, sl: (bi, gi, 0, qi), memory_space=pltpu.SMEM),
            pl.BlockSpec((1, qb, GROUP_Q_WIDTH), lambda bi, gi, qi, sl: (bi, qi, COL_Q // GROUP_Q_WIDTH + gi)),
            pl.BlockSpec((1, s, HEAD_DIM), lambda bi, gi, qi, sl: (bi, 0, COL_KS // HEAD_DIM + gi)),
            pl.BlockSpec((1, n_slc, SLC_LEN, HEAD_DIM), lambda bi, gi, qi, sl: (bi, 0, 0, COL_VS // HEAD_DIM + gi)),
        ],
        out_specs=pl.BlockSpec((1, qb, GROUP_Q_WIDTH), lambda bi, gi, qi, sl: (bi, qi, gi)),
        scratch_shapes=[
            pltpu.VMEM((qb * Q_ROWS, HEAD_DIM), F32),
            pltpu.VMEM((qb * NSA_REP, HEAD_DIM), F32),
            pltpu.VMEM((n_slc, KT_ROWS, LANES), BF16),
            pltpu.VMEM((2, GRP, Q_ROWS, top_n * SLC_LEN), F32),
            pltpu.VMEM((2, GRP, Q_ROWS, top_n * SLC_LEN), F32),
        ],
    )
    return pl.pallas_call(
        functools.partial(_slc_kernel, top_n=top_n, seq=s, qb=qb),
        grid_spec=grid_spec,
        out_shape=jax.ShapeDtypeStruct((b, s, NSA_WIDTH), F32),
        compiler_params=_params(("arbitrary", "arbitrary", "arbitrary")),
        name="slc_attn",
    )(slopes, idx, proj3, proj3, proj4)


def _win_kernel(slopes_ref, q_ref, k_ref, v_ref, oc_ref, os_ref, gt_ref, y_ref, *, span):
    g = pl.program_id(1)
    q0 = pl.program_id(2) * Q_BLOCK
    start = pl.multiple_of(jnp.maximum(q0 - WINDOW, 0), Q_BLOCK)
    kw = k_ref[0, pl.ds(start, span), :]
    vw = v_ref[0, pl.ds(start, span), :]
    t = q0 + lax.broadcasted_iota(jnp.int32, (Q_BLOCK, 1), 0)
    s_pos = start + lax.broadcasted_iota(jnp.int32, (1, span), 1)
    rel = t - s_pos
    mask = (rel >= 0) & (rel < WINDOW)
    rel_f = rel.astype(F32)
    gates = jax.nn.sigmoid(gt_ref[...])
    for h in range(NSA_REP):
        cs = slice(h * HEAD_DIM, (h + 1) * HEAD_DIM)
        slope = slopes_ref[g * NSA_REP + h]
        s = _dot_nt(q_ref[0, :, cs], kw) * ATTN_SCALE - slope * rel_f
        p = _masked_softmax(s, mask)
        ow = jnp.dot(p.astype(BF16), vw, preferred_element_type=F32)
        c = h * N_BRANCH
        y = gates[:, c:c + 1] * oc_ref[0, :, cs] + gates[:, c + 1:c + 2] * os_ref[0, :, cs] + gates[:, c + 2:c + 3] * ow
        y_ref[0, :, cs] = y.astype(y_ref.dtype)


def _win_combine(proj3, o_cmp, o_slc, gates_raw, slopes):
    b, s, _ = proj3.shape
    g = NSA_KV_GROUPS
    nq = s // Q_BLOCK
    span = min(WINDOW + Q_BLOCK, s)
    grid_spec = pltpu.PrefetchScalarGridSpec(
        num_scalar_prefetch=1,
        grid=(b, g, nq),
        in_specs=[
            pl.BlockSpec((1, Q_BLOCK, GROUP_Q_WIDTH), lambda bi, gi, qi, sl: (bi, qi, COL_Q // GROUP_Q_WIDTH + gi)),
            pl.BlockSpec((1, s, HEAD_DIM), lambda bi, gi, qi, sl: (bi, 0, COL_KW // HEAD_DIM + gi)),
            pl.BlockSpec((1, s, HEAD_DIM), lambda bi, gi, qi, sl: (bi, 0, COL_VW // HEAD_DIM + gi)),
            pl.BlockSpec((1, Q_BLOCK, GROUP_Q_WIDTH), lambda bi, gi, qi, sl: (bi, qi, gi)),
            pl.BlockSpec((1, Q_BLOCK, GROUP_Q_WIDTH), lambda bi, gi, qi, sl: (bi, qi, gi)),
            pl.BlockSpec((Q_BLOCK, LANES), lambda bi, gi, qi, sl: (bi * nq + qi, gi)),
        ],
        out_specs=pl.BlockSpec((1, Q_BLOCK, GROUP_Q_WIDTH), lambda bi, gi, qi, sl: (bi, qi, gi)),
    )
    return pl.pallas_call(
        functools.partial(_win_kernel, span=span),
        grid_spec=grid_spec,
        out_shape=jax.ShapeDtypeStruct((b, s, NSA_WIDTH), BF16),
        compiler_params=_params(("arbitrary", "arbitrary", "arbitrary")),
        name="win_attn_combine",
    )(slopes, proj3, proj3, proj3, o_cmp, o_slc, gates_raw)


def _matmul_res_kernel(*refs, n_pairs):
    r_ref = refs[2 * n_pairs]
    o_ref = refs[2 * n_pairs + 1]
    acc = r_ref[...]
    for i in range(n_pairs):
        acc = acc + jnp.dot(refs[i][...], refs[n_pairs + i][...], preferred_element_type=F32)
    o_ref[...] = acc


def _matmul_res(acts, weights, resid, tm, tn):
    m, n = resid.shape
    n_pairs = len(acts)
    in_specs = [pl.BlockSpec((tm, a.shape[1]), lambda i, j: (i, 0)) for a in acts]
    in_specs += [pl.BlockSpec((w.shape[0], tn), lambda i, j: (0, j)) for w in weights]
    in_specs += [pl.BlockSpec((tm, tn), lambda i, j: (i, j))]
    return pl.pallas_call(
        functools.partial(_matmul_res_kernel, n_pairs=n_pairs),
        grid=(m // tm, n // tn),
        in_specs=in_specs,
        out_specs=pl.BlockSpec((tm, tn), lambda i, j: (i, j)),
        out_shape=jax.ShapeDtypeStruct((m, n), F32),
        compiler_params=_params(("arbitrary", "arbitrary")),
        name="matmul_residual",
    )(*acts, *weights, resid)


def _ffn_kernel(h_ref, g_ref, wg_ref, wu_ref, wd_ref, gf_ref, o_ref, xn_ref, acc_ref, *, final_norm):
    k = pl.program_id(1)

    @pl.when(k == 0)
    def _():
        xn_ref[...] = _rms_scale(h_ref[...], g_ref[...]).astype(BF16)
        acc_ref[...] = jnp.zeros_like(acc_ref)

    xn = xn_ref[...]
    a = jnp.dot(xn, wg_ref[...], preferred_element_type=F32)
    u = jnp.dot(xn, wu_ref[...], preferred_element_type=F32)
    act = (a * jax.nn.sigmoid(a) * u).astype(BF16)
    acc_ref[...] += jnp.dot(act, wd_ref[...], preferred_element_type=F32)

    @pl.when(k == pl.num_programs(1) - 1)
    def _():
        out = h_ref[...] + acc_ref[...]
        if final_norm:
            out = _rms_scale(out, gf_ref[...])
        o_ref[...] = out


def _ffn(h, gain, w_gate, w_up, w_down, gain_final, final_norm, tm, tk):
    m, d = h.shape
    dff = w_gate.shape[1]
    return pl.pallas_call(
        functools.partial(_ffn_kernel, final_norm=final_norm),
        grid=(m // tm, dff // tk),
        in_specs=[
            pl.BlockSpec((tm, d), lambda i, k: (i, 0)),
            pl.BlockSpec((1, d), lambda i, k: (0, 0)),
            pl.BlockSpec((d, tk), lambda i, k: (0, k)),
            pl.BlockSpec((d, tk), lambda i, k: (0, k)),
            pl.BlockSpec((tk, d), lambda i, k: (k, 0)),
            pl.BlockSpec((1, d), lambda i, k: (0, 0)),
        ],
        out_specs=pl.BlockSpec((tm, d), lambda i, k: (i, 0)),
        out_shape=jax.ShapeDtypeStruct((m, d), F32),
        scratch_shapes=[pltpu.VMEM((tm, d), BF16), pltpu.VMEM((tm, d), F32)],
        compiler_params=_params(("arbitrary", "arbitrary")),
        name="ffn_swiglu",
    )(h, gain.reshape(1, d), w_gate, w_up, w_down, gain_final.reshape(1, d))


TAIL_ROWS = 8


def _conv_in_kernel(x_ref, g_ref, wb_ref, wc_ref, wz_ref, cw_ref, y_ref, xn_ref, tail_ref, *, tiles_per_seq):
    i = pl.program_id(0)
    j = pl.program_id(1)

    @pl.when(j == 0)
    def _():
        xn_ref[...] = _rms_scale(x_ref[...], g_ref[...]).astype(BF16)

    xn = xn_ref[...]
    bg = jnp.dot(xn, wb_ref[...], preferred_element_type=F32)
    z = jnp.dot(xn, wc_ref[...], preferred_element_type=F32) * jnp.dot(xn, wz_ref[...], preferred_element_type=F32)
    tm = z.shape[0]
    seq_start = (i % tiles_per_seq) == 0
    prev = jnp.where(seq_start, 0.0, tail_ref[j])
    tail_ref[j] = z[tm - TAIL_ROWS:, :]
    prev1 = prev[TAIL_ROWS - 1:TAIL_ROWS, :]
    prev2 = prev[TAIL_ROWS - 2:TAIL_ROWS - 1, :]
    row = lax.broadcasted_iota(jnp.int32, (tm, 1), 0)
    z1 = jnp.where(row >= 1, pltpu.roll(z, 1, 0), prev1)
    z2 = jnp.where(row >= 2, pltpu.roll(z, 2, 0), jnp.where(row == 1, prev1, prev2))
    zc = cw_ref[2:3, :] * z + cw_ref[1:2, :] * z1 + cw_ref[0:1, :] * z2
    y_ref[...] = (bg * zc).astype(y_ref.dtype)


def _conv_in(h, gain, w_in, conv_w, seq, tm, tn):
    m, d = h.shape
    cw = conv_w.shape[1]
    nj = cw // tn
    return pl.pallas_call(
        functools.partial(_conv_in_kernel, tiles_per_seq=seq // tm),
        grid=(m // tm, nj),
        in_specs=[
            pl.BlockSpec((tm, d), lambda i, j: (i, 0)),
            pl.BlockSpec((1, d), lambda i, j: (0, 0)),
            pl.BlockSpec((d, tn), lambda i, j: (0, j)),
            pl.BlockSpec((d, tn), lambda i, j: (0, nj + j)),
            pl.BlockSpec((d, tn), lambda i, j: (0, 2 * nj + j)),
            pl.BlockSpec((conv_w.shape[0], tn), lambda i, j: (0, j)),
        ],
        out_specs=pl.BlockSpec((tm, tn), lambda i, j: (i, j)),
        out_shape=jax.ShapeDtypeStruct((m, cw), BF16),
        scratch_shapes=[pltpu.VMEM((tm, d), BF16), pltpu.VMEM((nj, TAIL_ROWS, tn), F32)],
        compiler_params=_params(("arbitrary", "arbitrary")),
        name="conv_mixer_in",
    )(h, gain.reshape(1, d), w_in, w_in, w_in, conv_w)


def _gate_weight(w_in):
    per_group = NSA_REP * N_BRANCH
    cols = []
    for g in range(NSA_KV_GROUPS):
        blk = w_in[:, COL_GT + g * per_group:COL_GT + (g + 1) * per_group]
        cols.append(jnp.pad(blk, ((0, 0), (0, LANES - per_group))))
    return jnp.concatenate(cols, axis=1)


def _pick_tile(m, pref):
    t = min(m, pref)
    while m % t:
        t //= 2
    return t


def kernel(x, norm_mix, norm_ffn, norm_f, w_in_ab, w_out_ab, sgu_w, sgu_b, sgu_g, cmp_pe_k, cmp_w1_k, cmp_w2_k, cmp_pe_v, cmp_w1_v, cmp_w2_v, w_in_c, conv_w, w_out_c, w_gate, w_up, w_down):
    b, s, d = x.shape
    m = b * s
    tm = _pick_tile(m, ROW_TILE)
    tm_wide = _pick_tile(s, 2 * ROW_TILE)
    top_n = min(SLC_TOP_N, s // SLC_LEN)
    slopes = (2.0 ** (-8.0 * jnp.arange(1, NSA_HEADS + 1, dtype=F32) / NSA_HEADS)).astype(F32)
    imp_mat = jnp.asarray(_importance_matrix(s // CMP_STRIDE, s // SLC_LEN), BF16)

    h = x.reshape(m, d)

    w_in = w_in_ab[0]
    proj, gates_raw = _norm_matmul(h, norm_mix[0], w_in[:, :COL_GT].astype(BF16), _gate_weight(w_in).astype(BF16),
                                   tm_wide, COL_GT // 3)
    y_a = _sgu(proj, sgu_w[0], sgu_b[0], sgu_g[0], tm)
    proj3 = proj.reshape(b, s, COL_GT)
    cmp_kv = _compress(
        proj3,
        jnp.stack([cmp_pe_k[0], cmp_pe_v[0]]),
        jnp.stack([cmp_w1_k[0], cmp_w1_v[0]]).astype(BF16),
        jnp.stack([cmp_w2_k[0], cmp_w2_v[0]]).astype(BF16),
    )
    o_cmp, idx = _cmp_select(proj3, cmp_kv, slopes, imp_mat, top_n)
    o_slc = _slc_attention(proj3, idx, slopes, top_n)
    y_b = _win_combine(proj3, o_cmp, o_slc, gates_raw, slopes)
    w_out = w_out_ab[0].astype(BF16)
    h = _matmul_res([y_a, y_b.reshape(m, NSA_WIDTH)], [w_out[:SGU_WIDTH], w_out[SGU_WIDTH:]], h, tm, d)
    h = _ffn(h, norm_ffn[0], w_gate[0].astype(BF16), w_up[0].astype(BF16), w_down[0].astype(BF16), norm_f, False, tm, COL_TILE)

    y_c = _conv_in(h, norm_mix[1], w_in_c[0].astype(BF16), conv_w[0], s, tm_wide, COL_TILE)
    h = _matmul_res([y_c], [w_out_c[0].astype(BF16)], h, tm, d)
    h = _ffn(h, norm_ffn[1], w_gate[1].astype(BF16), w_up[1].astype(BF16), w_down[1].astype(BF16), norm_f, True, tm, COL_TILE)
    return h.reshape(b, s, d)
```

```python
import functools
import math

import jax
import jax.numpy as jnp
import numpy as np
from jax import lax
from jax.experimental import pallas as pl
from jax.experimental.pallas import tpu as pltpu

F32 = jnp.float32
BF16 = jnp.bfloat16

HEAD_DIM = 128
SGU_GROUPS = 8
SGU_CHUNK = 128
SGU_WIDTH = SGU_GROUPS * HEAD_DIM
NSA_HEADS = 8
NSA_KV_GROUPS = 2
NSA_REP = NSA_HEADS // NSA_KV_GROUPS
NSA_WIDTH = NSA_HEADS * HEAD_DIM
KV_WIDTH = NSA_KV_GROUPS * HEAD_DIM
GROUP_Q_WIDTH = NSA_REP * HEAD_DIM
CMP_LEN = 32
CMP_STRIDE = 16
SLC_LEN = 64
SLC_TOP_N = 16
WINDOW = 512
Q_BLOCK = 128
N_BRANCH = 3
FORCE_SCORE = 1e9
EPS = 1e-6
NEG = -1e30
REMOVED = -3e38
ATTN_SCALE = HEAD_DIM ** -0.5

COL_U = 0
COL_V = SGU_WIDTH
COL_Q = 2 * SGU_WIDTH
COL_KC = COL_Q + NSA_WIDTH
COL_VC = COL_KC + KV_WIDTH
COL_KS = COL_VC + KV_WIDTH
COL_VS = COL_KS + KV_WIDTH
COL_KW = COL_VS + KV_WIDTH
COL_VW = COL_KW + KV_WIDTH
COL_GT = COL_VW + KV_WIDTH

LANES = 128
VMEM_LIMIT = 56 * 1024 * 1024
ROW_TILE = 512
COL_TILE = 512


def _params(sem, vmem=VMEM_LIMIT):
    return pltpu.CompilerParams(dimension_semantics=sem, vmem_limit_bytes=vmem)


def _rms_scale(x, gain_row):
    ms = jnp.mean(x * x, axis=-1, keepdims=True)
    return x * lax.rsqrt(ms + EPS) * gain_row


def _masked_softmax(s, mask):
    s = jnp.where(mask, s, NEG)
    m = jnp.max(s, axis=-1, keepdims=True)
    p = jnp.exp(s - m) * mask.astype(F32)
    denom = jnp.maximum(jnp.sum(p, axis=-1, keepdims=True), 1e-30)
    return p * (1.0 / denom)


def _dot_nt(a, b):
    return lax.dot_general(a, b, (((1,), (1,)), ((), ())), preferred_element_type=F32)


def _norm_matmul_kernel(x_ref, g_ref, w_ref, ws_ref, o_ref, os_ref, xn_ref):
    @pl.when(pl.program_id(1) == 0)
    def _():
        xn_ref[...] = _rms_scale(x_ref[...], g_ref[...]).astype(BF16)
        os_ref[...] = jnp.dot(xn_ref[...], ws_ref[...], preferred_element_type=F32)

    o_ref[...] = jnp.dot(xn_ref[...], w_ref[...], preferred_element_type=F32).astype(o_ref.dtype)


def _norm_matmul(x, gain, w, w_side, tm, tn):
    m, d = x.shape
    n = w.shape[1]
    ns = w_side.shape[1]
    return pl.pallas_call(
        _norm_matmul_kernel,
        grid=(m // tm, n // tn),
        in_specs=[
            pl.BlockSpec((tm, d), lambda i, j: (i, 0)),
            pl.BlockSpec((1, d), lambda i, j: (0, 0)),
            pl.BlockSpec((d, tn), lambda i, j: (0, j)),
            pl.BlockSpec((d, ns), lambda i, j: (0, 0)),
        ],
        out_specs=[
            pl.BlockSpec((tm, tn), lambda i, j: (i, j)),
            pl.BlockSpec((tm, ns), lambda i, j: (i, 0)),
        ],
        out_shape=[jax.ShapeDtypeStruct((m, n), BF16), jax.ShapeDtypeStruct((m, ns), F32)],
        scratch_shapes=[pltpu.VMEM((tm, d), BF16)],
        compiler_params=_params(("arbitrary", "arbitrary")),
        name="norm_matmul",
    )(x, gain.reshape(1, d), w, w_side)


def _sgu_kernel(u_ref, v_ref, w_ref, bt_ref, gain_ref, o_ref, *, chunks):
    row = lax.broadcasted_iota(jnp.int32, (SGU_CHUNK, SGU_CHUNK), 0)
    col = lax.broadcasted_iota(jnp.int32, (SGU_CHUNK, SGU_CHUNK), 1)
    tri = row >= col
    for g in range(SGU_GROUPS):
        w = jnp.where(tri, w_ref[g], 0.0).astype(BF16)
        bias = bt_ref[:, g:g + 1]
        gain = gain_ref[g:g + 1, :]
        cs = slice(g * HEAD_DIM, (g + 1) * HEAD_DIM)
        for c in range(chunks):
            rs = slice(c * SGU_CHUNK, (c + 1) * SGU_CHUNK)
            u = jax.nn.gelu(u_ref[rs, cs].astype(F32))
            v = _rms_scale(jax.nn.gelu(v_ref[rs, cs].astype(F32)), gain)
            sg = jnp.dot(w, v.astype(BF16), preferred_element_type=F32) + bias
            o_ref[rs, cs] = (u * sg).astype(o_ref.dtype)


def _sgu(proj, sgu_w, sgu_b, sgu_g, tr):
    m = proj.shape[0]
    chunks = tr // SGU_CHUNK
    return pl.pallas_call(
        functools.partial(_sgu_kernel, chunks=chunks),
        grid=(m // tr,),
        in_specs=[
            pl.BlockSpec((tr, SGU_WIDTH), lambda i: (i, COL_U // SGU_WIDTH)),
            pl.BlockSpec((tr, SGU_WIDTH), lambda i: (i, COL_V // SGU_WIDTH)),
            pl.BlockSpec((SGU_GROUPS, SGU_CHUNK, SGU_CHUNK), lambda i: (0, 0, 0)),
            pl.BlockSpec((SGU_CHUNK, SGU_GROUPS), lambda i: (0, 0)),
            pl.BlockSpec((SGU_GROUPS, HEAD_DIM), lambda i: (0, 0)),
        ],
        out_specs=pl.BlockSpec((tr, SGU_WIDTH), lambda i: (i, 0)),
        out_shape=jax.ShapeDtypeStruct((m, SGU_WIDTH), BF16),
        compiler_params=_params(("arbitrary",)),
        name="sgu_mixer",
    )(proj, proj, sgu_w, sgu_b.T, sgu_g)


def _compress_kernel(x_ref, pe_ref, w1_ref, w2_ref, o_ref, xf_ref, *, seq):
    nc = seq // CMP_STRIDE
    xf_ref[0:seq, :] = x_ref[0].astype(F32)
    xf_ref[seq:seq + CMP_STRIDE, :] = jnp.zeros((CMP_STRIDE, HEAD_DIM), F32)
    acc = jnp.zeros((nc, HEAD_DIM), F32)
    for l in range(CMP_LEN):
        rows = xf_ref[pl.ds(l, nc, stride=CMP_STRIDE), :] + pe_ref[0, l:l + 1, :]
        acc = acc + jnp.dot(rows.astype(BF16), w1_ref[0, l], preferred_element_type=F32)
    hid = jax.nn.gelu(acc)
    o_ref[0, 0] = jnp.dot(hid.astype(BF16), w2_ref[0], preferred_element_type=F32).astype(o_ref.dtype)


def _compress(proj3, pe, w1, w2):
    b, s, _ = proj3.shape
    nc = s // CMP_STRIDE
    g = NSA_KV_GROUPS
    col0 = COL_KC // HEAD_DIM
    return pl.pallas_call(
        functools.partial(_compress_kernel, seq=s),
        grid=(2, b, g),
        in_specs=[
            pl.BlockSpec((1, s, HEAD_DIM), lambda kv, bi, gi: (bi, 0, col0 + g * kv + gi)),
            pl.BlockSpec((1, CMP_LEN, HEAD_DIM), lambda kv, bi, gi: (kv, 0, 0)),
            pl.BlockSpec((1, CMP_LEN, HEAD_DIM, HEAD_DIM), lambda kv, bi, gi: (kv, 0, 0, 0)),
            pl.BlockSpec((1, HEAD_DIM, HEAD_DIM), lambda kv, bi, gi: (kv, 0, 0)),
        ],
        out_specs=pl.BlockSpec((1, 1, nc, HEAD_DIM), lambda kv, bi, gi: (kv, bi * g + gi, 0, 0)),
        out_shape=jax.ShapeDtypeStruct((2, b * g, nc, HEAD_DIM), BF16),
        scratch_shapes=[pltpu.VMEM((s + CMP_STRIDE, HEAD_DIM), F32)],
        compiler_params=_params(("arbitrary", "arbitrary", "arbitrary")),
        name="kv_compress",
    )(proj3, pe, w1, w2)


def _importance_matrix(nc, n_slc):
    r1 = SLC_LEN // CMP_STRIDE
    r2 = CMP_LEN // CMP_STRIDE
    mat = np.zeros((nc, n_slc), np.float32)
    for j in range(n_slc):
        for m_ in range(r1):
            for n_ in range(r2):
                n = r1 * j + m_ - n_
                if 0 <= n < nc:
                    mat[n, j] += 1.0
    return mat


CMP_BAND = 256


def _cmp_select_kernel(slopes_ref, q_ref, kc_ref, vc_ref, m_ref, oc_ref, idx_ref, imp_ref, *, nc, n_slc, top_n):
    g = pl.program_id(1)
    q0 = pl.program_id(2) * Q_BLOCK
    t = q0 + lax.broadcasted_iota(jnp.int32, (Q_BLOCK, 1), 0)
    band_cols = min(CMP_BAND, nc)
    n_bands = nc // band_cols
    band = ((q0 + Q_BLOCK) // CMP_STRIDE - 1) // band_cols

    def attend(cols):
        n_ix = lax.broadcasted_iota(jnp.int32, (1, cols), 1)
        valid = (n_ix * CMP_STRIDE + (CMP_LEN - 1)) <= t
        dist = t.astype(F32) - ((n_ix * CMP_STRIDE).astype(F32) + (CMP_LEN - 1) / 2)
        kc = kc_ref[0, 0, :cols, :]
        vc = vc_ref[0, 0, :cols, :]
        psum = jnp.zeros((Q_BLOCK, cols), F32)
        for h in range(NSA_REP):
            cs = slice(h * HEAD_DIM, (h + 1) * HEAD_DIM)
            slope = slopes_ref[g * NSA_REP + h]
            s = _dot_nt(q_ref[0, :, cs], kc) * ATTN_SCALE - slope * dist
            p = _masked_softmax(s, valid)
            psum = psum + p
            oc_ref[0, :, cs] = jnp.dot(p.astype(BF16), vc, preferred_element_type=F32).astype(oc_ref.dtype)
        hi = psum.astype(BF16)
        lo = (psum - hi.astype(F32)).astype(BF16)
        mat = m_ref[:cols, :]
        imp_ref[...] = jnp.dot(hi, mat, preferred_element_type=F32) + jnp.dot(lo, mat, preferred_element_type=F32)

    for k in range(n_bands):
        pl.when(band == k)(functools.partial(attend, (k + 1) * band_cols))

    imp_t = imp_ref[...].T
    t_row = q0 + lax.broadcasted_iota(jnp.int32, (1, Q_BLOCK), 1)
    j_ix = lax.broadcasted_iota(jnp.int32, (n_slc, 1), 0)
    cur = t_row // SLC_LEN
    causal = j_ix <= cur
    forced = (j_ix == 0) | (j_ix == cur) | (j_ix == cur - 1)
    score = jnp.where(forced, FORCE_SCORE, jnp.where(causal, imp_t, -1.0))
    j_f = j_ix.astype(F32)
    r_ix = lax.broadcasted_iota(jnp.int32, (top_n, 1), 0)
    idx = jnp.zeros((top_n, Q_BLOCK), jnp.int32)
    for r in range(top_n):
        best = jnp.max(score, axis=0, keepdims=True)
        pick = jnp.min(jnp.where(score == best, j_f, float(n_slc)), axis=0, keepdims=True)
        score = jnp.where(j_f == pick, REMOVED, score)
        idx = jnp.where(r_ix == r, pick.astype(jnp.int32), idx)
    idx_ref[0, 0] = idx


def _cmp_select(proj3, cmp_kv, slopes, imp_mat, top_n):
    b, s, _ = proj3.shape
    g = NSA_KV_GROUPS
    nc = s // CMP_STRIDE
    n_slc = s // SLC_LEN
    grid_spec = pltpu.PrefetchScalarGridSpec(
        num_scalar_prefetch=1,
        grid=(b, g, s // Q_BLOCK),
        in_specs=[
            pl.BlockSpec((1, Q_BLOCK, GROUP_Q_WIDTH), lambda bi, gi, qi, sl: (bi, qi, COL_Q // GROUP_Q_WIDTH + gi)),
            pl.BlockSpec((1, 1, nc, HEAD_DIM), lambda bi, gi, qi, sl: (0, bi * g + gi, 0, 0)),
            pl.BlockSpec((1, 1, nc, HEAD_DIM), lambda bi, gi, qi, sl: (1, bi * g + gi, 0, 0)),
            pl.BlockSpec((nc, n_slc), lambda bi, gi, qi, sl: (0, 0)),
        ],
        out_specs=[
            pl.BlockSpec((1, Q_BLOCK, GROUP_Q_WIDTH), lambda bi, gi, qi, sl: (bi, qi, gi)),
            pl.BlockSpec((1, 1, top_n, Q_BLOCK), lambda bi, gi, qi, sl: (bi, gi, 0, qi)),
        ],
        scratch_shapes=[pltpu.VMEM((Q_BLOCK, n_slc), F32)],
    )
    return pl.pallas_call(
        functools.partial(_cmp_select_kernel, nc=nc, n_slc=n_slc, top_n=top_n),
        grid_spec=grid_spec,
        out_shape=[
            jax.ShapeDtypeStruct((b, s, NSA_WIDTH), F32),
            jax.ShapeDtypeStruct((b, g, top_n, s), jnp.int32),
        ],
        compiler_params=_params(("arbitrary", "arbitrary", "arbitrary")),
        name="cmp_attn_select",
    )(slopes, proj3, cmp_kv, cmp_kv, imp_mat)


KT_ROWS = HEAD_DIM + 16
POS_HI = float(LANES)
Q_ROWS = 8
GRP = 4
SLC_Q_TILE = 256
SLC_UNROLL = 6


def _slc_kernel(slopes_ref, idx_ref, q_ref, k_ref, v_ref, o_ref, qs_ref, os_ref, kt_ref, s_buf, e_buf, *, top_n, seq, qb):
    g = pl.program_id(1)
    q0 = pl.program_id(2) * qb
    n_tiles = top_n * SLC_LEN // LANES
    n_grp = qb // GRP

    @pl.when(pl.program_id(2) == 0)
    def _():
        sub = lax.broadcasted_iota(jnp.int32, (KT_ROWS - HEAD_DIM, LANES), 0)
        lane_f = lax.broadcasted_iota(jnp.int32, (KT_ROWS - HEAD_DIM, LANES), 1).astype(F32)

        def tr_body(i, carry):
            r0 = pl.multiple_of(i * LANES, LANES)
            tile = k_ref[0, pl.ds(r0, LANES), :].astype(F32).T
            pos_rows = jnp.where(sub == 0, lax.convert_element_type(i, F32), jnp.where(sub == 1, lane_f, 0.0))
            ext = jnp.concatenate([tile, pos_rows], axis=0)
            kt_ref[2 * i] = ext.astype(BF16)
            kt_ref[2 * i + 1] = pltpu.roll(ext, SLC_LEN, 1).astype(BF16)
            return carry
        lax.fori_loop(0, seq // LANES, tr_body, 0)
        s_buf[...] = jnp.zeros_like(s_buf)
        e_buf[...] = jnp.zeros_like(e_buf)
        qs_ref[...] = jnp.zeros_like(qs_ref)

    for h in range(NSA_REP):
        qs_ref[pl.ds(h, qb, stride=Q_ROWS), :] = q_ref[0, :, h * HEAD_DIM:(h + 1) * HEAD_DIM].astype(F32)
    row = lax.broadcasted_iota(jnp.int32, (Q_ROWS, 1), 0)
    low_rows = row < NSA_REP
    e_row = lax.broadcasted_iota(jnp.int32, (Q_ROWS, KT_ROWS - HEAD_DIM), 0)
    e_col = lax.broadcasted_iota(jnp.int32, (Q_ROWS, KT_ROWS - HEAD_DIM), 1)
    slope_rows = []
    pos_cols = []
    pos_row = [NSA_REP, 0]
    for par in range(2):
        sl = jnp.zeros((Q_ROWS, 1), F32)
        for h in range(NSA_REP):
            sl = jnp.where(row == h + par * NSA_REP, slopes_ref[g * NSA_REP + h], sl)
        slope_rows.append(sl)
        pos_cols.append(jnp.where((e_row == pos_row[par]) & (e_col == 0), POS_HI,
                                  jnp.where((e_row == pos_row[par]) & (e_col == 1), 1.0, 0.0)).astype(BF16))
    low_half = lax.broadcasted_iota(jnp.int32, (KT_ROWS, LANES), 1) < SLC_LEN

    def block_ids(tloc):
        return [idx_ref[0, 0, r, tloc] for r in range(top_n)]

    def stage_scores(grp, slot):
        for qi in range(GRP):
            par = qi % 2
            tloc = grp * GRP + qi
            ids = block_ids(tloc)
            q_rows = qs_ref[pl.ds(pl.multiple_of(tloc * Q_ROWS, Q_ROWS), Q_ROWS), :]
            if par:
                q_rows = pltpu.roll(q_rows, NSA_REP, 0)
            lhs = jnp.concatenate([q_rows.astype(BF16), pos_cols[par]], axis=1)
            tiles = [jnp.where(low_half, kt_ref[ids[2 * c]], kt_ref[ids[2 * c + 1] ^ 1]) for c in range(n_tiles)]
            s_buf[slot, qi] = jnp.dot(lhs, jnp.concatenate(tiles, axis=1), preferred_element_type=F32)

    def stage_softmax(grp, slot):
        for qi in range(GRP):
            par = qi % 2
            t_f = lax.convert_element_type(q0 + grp * GRP + qi, F32)
            raw = s_buf[slot, qi]
            key_pos = raw[pos_row[par]:pos_row[par] + 1, :]
            mask = key_pos <= t_f
            s = raw * ATTN_SCALE - slope_rows[par] * (t_f - key_pos)
            e_buf[slot, qi] = _masked_softmax(s, mask)

    def stage_values(grp, slot):
        for w in range(GRP // 2):
            tloc = grp * GRP + 2 * w
            v_pair = []
            for par in range(2):
                v_pair.append(jnp.concatenate([v_ref[0, j] for j in block_ids(tloc + par)], axis=0))
            prob = jnp.where(low_rows, e_buf[slot, 2 * w], e_buf[slot, 2 * w + 1]).astype(BF16)
            acc = jnp.dot(prob, jnp.concatenate(v_pair, axis=1), preferred_element_type=F32)
            os_ref[pl.ds(pl.multiple_of(tloc * NSA_REP, 2 * NSA_REP), 2 * NSA_REP), :] = jnp.where(
                low_rows, acc[:, :HEAD_DIM], acc[:, HEAD_DIM:])

    def body(it, carry):
        cur = it % 2
        stage_values(jnp.clip(it - 2, 0, n_grp - 1), cur)
        stage_softmax(jnp.clip(it - 1, 0, n_grp - 1), 1 - cur)
        stage_scores(jnp.minimum(it, n_grp - 1), cur)
        return carry

    trips = n_grp + 2
    lax.fori_loop(0, trips, body, 0, unroll=math.gcd(trips, SLC_UNROLL))
    for h in range(NSA_REP):
        o_ref[0, :, h * HEAD_DIM:(h + 1) * HEAD_DIM] = os_ref[pl.ds(h, qb, stride=NSA_REP), :].astype(o_ref.dtype)


def _slc_attention(proj3, idx, slopes, top_n):
    b, s, width = proj3.shape
    g = NSA_KV_GROUPS
    n_slc = s // SLC_LEN
    qb = _pick_tile(s, SLC_Q_TILE)
    assert top_n % (LANES // SLC_LEN) == 0 and qb % GRP == 0 and GRP % 2 == 0
    proj4 = proj3.reshape(b, n_slc, SLC_LEN, width)
    grid_spec = pltpu.PrefetchScalarGridSpec(
        num_scalar_prefetch=1,
        grid=(b, g, s // qb),
        in_specs=[
            pl.BlockSpec((1, 1, top_n, qb), lambda bi, gi, qi, sl: (bi, gi, 0, qi), memory_space=pltpu.SMEM),
            pl.BlockSpec((1, qb, GROUP_Q_WIDTH), lambda bi, gi, qi, sl: (bi, qi, COL_Q // GROUP_Q_WIDTH + gi)),
            pl.BlockSpec((1, s, HEAD_DIM), lambda bi, gi, qi, sl: (bi, 0, COL_KS // HEAD_DIM + gi)),
            pl.BlockSpec((1, n_slc, SLC_LEN, HEAD_DIM), lambda bi, gi, qi, sl: (bi, 0, 0, COL_VS // HEAD_DIM + gi)),
        ],
        out_specs=pl.BlockSpec((1, qb, GROUP_Q_WIDTH), lambda bi, gi, qi, sl: (bi, qi, gi)),
        scratch_shapes=[
            pltpu.VMEM((qb * Q_ROWS, HEAD_DIM), F32),
            pltpu.VMEM((qb * NSA_REP, HEAD_DIM), F32),
            pltpu.VMEM((n_slc, KT_ROWS, LANES), BF16),
            pltpu.VMEM((2, GRP, Q_ROWS, top_n * SLC_LEN), F32),
            pltpu.VMEM((2, GRP, Q_ROWS, top_n * SLC_LEN), F32),
        ],
    )
    return pl.pallas_call(
        functools.partial(_slc_kernel, top_n=top_n, seq=s, qb=qb),
        grid_spec=grid_spec,
        out_shape=jax.ShapeDtypeStruct((b, s, NSA_WIDTH), F32),
        compiler_params=_params(("arbitrary", "arbitrary", "arbitrary")),
        name="slc_attn",
    )(slopes, idx, proj3, proj3, proj4)


def _win_kernel(slopes_ref, q_ref, k_ref, v_ref, oc_ref, os_ref, gt_ref, y_ref, *, span):
    g = pl.program_id(1)
    q0 = pl.program_id(2) * Q_BLOCK
    start = pl.multiple_of(jnp.maximum(q0 - WINDOW, 0), Q_BLOCK)
    kw = k_ref[0, pl.ds(start, span), :]
    vw = v_ref[0, pl.ds(start, span), :]
    t = q0 + lax.broadcasted_iota(jnp.int32, (Q_BLOCK, 1), 0)
    s_pos = start + lax.broadcasted_iota(jnp.int32, (1, span), 1)
    rel = t - s_pos
    mask = (rel >= 0) & (rel < WINDOW)
    rel_f = rel.astype(F32)
    gates = jax.nn.sigmoid(gt_ref[...])
    for h in range(NSA_REP):
        cs = slice(h * HEAD_DIM, (h + 1) * HEAD_DIM)
        slope = slopes_ref[g * NSA_REP + h]
        s = _dot_nt(q_ref[0, :, cs], kw) * ATTN_SCALE - slope * rel_f
        p = _masked_softmax(s, mask)
        ow = jnp.dot(p.astype(BF16), vw, preferred_element_type=F32)
        c = h * N_BRANCH
        y = gates[:, c:c + 1] * oc_ref[0, :, cs] + gates[:, c + 1:c + 2] * os_ref[0, :, cs] + gates[:, c + 2:c + 3] * ow
        y_ref[0, :, cs] = y.astype(y_ref.dtype)


def _win_combine(proj3, o_cmp, o_slc, gates_raw, slopes):
    b, s, _ = proj3.shape
    g = NSA_KV_GROUPS
    nq = s // Q_BLOCK
    span = min(WINDOW + Q_BLOCK, s)
    grid_spec = pltpu.PrefetchScalarGridSpec(
        num_scalar_prefetch=1,
        grid=(b, g, nq),
        in_specs=[
            pl.BlockSpec((1, Q_BLOCK, GROUP_Q_WIDTH), lambda bi, gi, qi, sl: (bi, qi, COL_Q // GROUP_Q_WIDTH + gi)),
            pl.BlockSpec((1, s, HEAD_DIM), lambda bi, gi, qi, sl: (bi, 0, COL_KW // HEAD_DIM + gi)),
            pl.BlockSpec((1, s, HEAD_DIM), lambda bi, gi, qi, sl: (bi, 0, COL_VW // HEAD_DIM + gi)),
            pl.BlockSpec((1, Q_BLOCK, GROUP_Q_WIDTH), lambda bi, gi, qi, sl: (bi, qi, gi)),
            pl.BlockSpec((1, Q_BLOCK, GROUP_Q_WIDTH), lambda bi, gi, qi, sl: (bi, qi, gi)),
            pl.BlockSpec((Q_BLOCK, LANES), lambda bi, gi, qi, sl: (bi * nq + qi, gi)),
        ],
        out_specs=pl.BlockSpec((1, Q_BLOCK, GROUP_Q_WIDTH), lambda bi, gi, qi, sl: (bi, qi, gi)),
    )
    return pl.pallas_call(
        functools.partial(_win_kernel, span=span),
        grid_spec=grid_spec,
        out_shape=jax.ShapeDtypeStruct((b, s, NSA_WIDTH), BF16),
        compiler_params=_params(("arbitrary", "arbitrary", "arbitrary")),
        name="win_attn_combine",
    )(slopes, proj3, proj3, proj3, o_cmp, o_slc, gates_raw)


def _matmul_res_kernel(*refs, n_pairs):
    r_ref = refs[2 * n_pairs]
    o_ref = refs[2 * n_pairs + 1]
    acc = r_ref[...]
    for i in range(n_pairs):
        acc = acc + jnp.dot(refs[i][...], refs[n_pairs + i][...], preferred_element_type=F32)
    o_ref[...] = acc


def _matmul_res(acts, weights, resid, tm, tn):
    m, n = resid.shape
    n_pairs = len(acts)
    in_specs = [pl.BlockSpec((tm, a.shape[1]), lambda i, j: (i, 0)) for a in acts]
    in_specs += [pl.BlockSpec((w.shape[0], tn), lambda i, j: (0, j)) for w in weights]
    in_specs += [pl.BlockSpec((tm, tn), lambda i, j: (i, j))]
    return pl.pallas_call(
        functools.partial(_matmul_res_kernel, n_pairs=n_pairs),
        grid=(m // tm, n // tn),
        in_specs=in_specs,
        out_specs=pl.BlockSpec((tm, tn), lambda i, j: (i, j)),
        out_shape=jax.ShapeDtypeStruct((m, n), F32),
        compiler_params=_params(("arbitrary", "arbitrary")),
        name="matmul_residual",
    )(*acts, *weights, resid)


def _ffn_kernel(h_ref, g_ref, wg_ref, wu_ref, wd_ref, gf_ref, o_ref, xn_ref, acc_ref, *, final_norm):
    k = pl.program_id(1)

    @pl.when(k == 0)
    def _():
        xn_ref[...] = _rms_scale(h_ref[...], g_ref[...]).astype(BF16)
        acc_ref[...] = jnp.zeros_like(acc_ref)

    xn = xn_ref[...]
    a = jnp.dot(xn, wg_ref[...], preferred_element_type=F32)
    u = jnp.dot(xn, wu_ref[...], preferred_element_type=F32)
    act = (a * jax.nn.sigmoid(a) * u).astype(BF16)
    acc_ref[...] += jnp.dot(act, wd_ref[...], preferred_element_type=F32)

    @pl.when(k == pl.num_programs(1) - 1)
    def _():
        out = h_ref[...] + acc_ref[...]
        if final_norm:
            out = _rms_scale(out, gf_ref[...])
        o_ref[...] = out


def _ffn(h, gain, w_gate, w_up, w_down, gain_final, final_norm, tm, tk):
    m, d = h.shape
    dff = w_gate.shape[1]
    return pl.pallas_call(
        functools.partial(_ffn_kernel, final_norm=final_norm),
        grid=(m // tm, dff // tk),
        in_specs=[
            pl.BlockSpec((tm, d), lambda i, k: (i, 0)),
            pl.BlockSpec((1, d), lambda i, k: (0, 0)),
            pl.BlockSpec((d, tk), lambda i, k: (0, k)),
            pl.BlockSpec((d, tk), lambda i, k: (0, k)),
            pl.BlockSpec((tk, d), lambda i, k: (k, 0)),
            pl.BlockSpec((1, d), lambda i, k: (0, 0)),
        ],
        out_specs=pl.BlockSpec((tm, d), lambda i, k: (i, 0)),
        out_shape=jax.ShapeDtypeStruct((m, d), F32),
        scratch_shapes=[pltpu.VMEM((tm, d), BF16), pltpu.VMEM((tm, d), F32)],
        compiler_params=_params(("arbitrary", "arbitrary")),
        name="ffn_swiglu",
    )(h, gain.reshape(1, d), w_gate, w_up, w_down, gain_final.reshape(1, d))


TAIL_ROWS = 8


def _conv_in_kernel(x_ref, g_ref, wb_ref, wc_ref, wz_ref, cw_ref, y_ref, xn_ref, tail_ref, *, tiles_per_seq):
    i = pl.program_id(0)
    j = pl.program_id(1)

    @pl.when(j == 0)
    def _():
        xn_ref[...] = _rms_scale(x_ref[...], g_ref[...]).astype(BF16)

    xn = xn_ref[...]
    bg = jnp.dot(xn, wb_ref[...], preferred_element_type=F32)
    z = jnp.dot(xn, wc_ref[...], preferred_element_type=F32) * jnp.dot(xn, wz_ref[...], preferred_element_type=F32)
    tm = z.shape[0]
    seq_start = (i % tiles_per_seq) == 0
    prev = jnp.where(seq_start, 0.0, tail_ref[j])
    tail_ref[j] = z[tm - TAIL_ROWS:, :]
    prev1 = prev[TAIL_ROWS - 1:TAIL_ROWS, :]
    prev2 = prev[TAIL_ROWS - 2:TAIL_ROWS - 1, :]
    row = lax.broadcasted_iota(jnp.int32, (tm, 1), 0)
    z1 = jnp.where(row >= 1, pltpu.roll(z, 1, 0), prev1)
    z2 = jnp.where(row >= 2, pltpu.roll(z, 2, 0), jnp.where(row == 1, prev1, prev2))
    zc = cw_ref[2:3, :] * z + cw_ref[1:2, :] * z1 + cw_ref[0:1, :] * z2
    y_ref[...] = (bg * zc).astype(y_ref.dtype)


def _conv_in(h, gain, w_in, conv_w, seq, tm, tn):
    m, d = h.shape
    cw = conv_w.shape[1]
    nj = cw // tn
    return pl.pallas_call(
        functools.partial(_conv_in_kernel, tiles_per_seq=seq // tm),
        grid=(m // tm, nj),
        in_specs=[
            pl.BlockSpec((tm, d), lambda i, j: (i, 0)),
            pl.BlockSpec((1, d), lambda i, j: (0, 0)),
            pl.BlockSpec((d, tn), lambda i, j: (0, j)),
            pl.BlockSpec((d, tn), lambda i, j: (0, nj + j)),
            pl.BlockSpec((d, tn), lambda i, j: (0, 2 * nj + j)),
            pl.BlockSpec((conv_w.shape[0], tn), lambda i, j: (0, j)),
        ],
        out_specs=pl.BlockSpec((tm, tn), lambda i, j: (i, j)),
        out_shape=jax.ShapeDtypeStruct((m, cw), BF16),
        scratch_shapes=[pltpu.VMEM((tm, d), BF16), pltpu.VMEM((nj, TAIL_ROWS, tn), F32)],
        compiler_params=_params(("arbitrary", "arbitrary")),
        name="conv_mixer_in",
    )(h, gain.reshape(1, d), w_in, w_in, w_in, conv_w)


def _gate_weight(w_in):
    per_group = NSA_REP * N_BRANCH
    cols = []
    for g in range(NSA_KV_GROUPS):
        blk = w_in[:, COL_GT + g * per_group:COL_GT + (g + 1) * per_group]
        cols.append(jnp.pad(blk, ((0, 0), (0, LANES - per_group))))
    return jnp.concatenate(cols, axis=1)


def _pick_tile(m, pref):
    t = min(m, pref)
    while m % t:
        t //= 2
    return t


def kernel(x, norm_mix, norm_ffn, norm_f, w_in_ab, w_out_ab, sgu_w, sgu_b, sgu_g, cmp_pe_k, cmp_w1_k, cmp_w2_k, cmp_pe_v, cmp_w1_v, cmp_w2_v, w_in_c, conv_w, w_out_c, w_gate, w_up, w_down):
    b, s, d = x.shape
    m = b * s
    tm = _pick_tile(m, ROW_TILE)
    tm_wide = _pick_tile(s, 2 * ROW_TILE)
    top_n = min(SLC_TOP_N, s // SLC_LEN)
    slopes = (2.0 ** (-8.0 * jnp.arange(1, NSA_HEADS + 1, dtype=F32) / NSA_HEADS)).astype(F32)
    imp_mat = jnp.asarray(_importance_matrix(s // CMP_STRIDE, s // SLC_LEN), BF16)

    h = x.reshape(m, d)

    w_in = w_in_ab[0]
    proj, gates_raw = _norm_matmul(h, norm_mix[0], w_in[:, :COL_GT].astype(BF16), _gate_weight(w_in).astype(BF16),
                                   tm_wide, COL_GT // 3)
    y_a = _sgu(proj, sgu_w[0], sgu_b[0], sgu_g[0], tm)
    proj3 = proj.reshape(b, s, COL_GT)
    cmp_kv = _compress(
        proj3,
        jnp.stack([cmp_pe_k[0], cmp_pe_v[0]]),
        jnp.stack([cmp_w1_k[0], cmp_w1_v[0]]).astype(BF16),
        jnp.stack([cmp_w2_k[0], cmp_w2_v[0]]).astype(BF16),
    )
    o_cmp, idx = _cmp_select(proj3, cmp_kv, slopes, imp_mat, top_n)
    o_slc = _slc_attention(proj3, idx, slopes, top_n)
    y_b = _win_combine(proj3, o_cmp, o_slc, gates_raw, slopes)
    w_out = w_out_ab[0].astype(BF16)
    h = _matmul_res([y_a, y_b.reshape(m, NSA_WIDTH)], [w_out[:SGU_WIDTH], w_out[SGU_WIDTH:]], h, tm, d)
    h = _ffn(h, norm_ffn[0], w_gate[0].astype(BF16), w_up[0].astype(BF16), w_down[0].astype(BF16), norm_f, False, tm, COL_TILE)

    y_c = _conv_in(h, norm_mix[1], w_in_c[0].astype(BF16), conv_w[0], s, tm_wide, COL_TILE)
    h = _matmul_res([y_c], [w_out_c[0].astype(BF16)], h, tm, d)
    h = _ffn(h, norm_ffn[1], w_gate[1].astype(BF16), w_up[1].astype(BF16), w_down[1].astype(BF16), norm_f, True, tm, COL_TILE)
    return h.reshape(b, s, d)
```

```python
import functools
import math

import jax
import jax.numpy as jnp
import numpy as np
from jax import lax
from jax.experimental import pallas as pl
from jax.experimental.pallas import tpu as pltpu

F32 = jnp.float32
BF16 = jnp.bfloat16

HEAD_DIM = 128
SGU_GROUPS = 8
SGU_CHUNK = 128
SGU_WIDTH = SGU_GROUPS * HEAD_DIM
NSA_HEADS = 8
NSA_KV_GROUPS = 2
NSA_REP = NSA_HEADS // NSA_KV_GROUPS
NSA_WIDTH = NSA_HEADS * HEAD_DIM
KV_WIDTH = NSA_KV_GROUPS * HEAD_DIM
GROUP_Q_WIDTH = NSA_REP * HEAD_DIM
CMP_LEN = 32
CMP_STRIDE = 16
SLC_LEN = 64
SLC_TOP_N = 16
WINDOW = 512
Q_BLOCK = 128
N_BRANCH = 3
FORCE_SCORE = 1e9
EPS = 1e-6
NEG = -1e30
REMOVED = -3e38
ATTN_SCALE = HEAD_DIM ** -0.5

COL_U = 0
COL_V = SGU_WIDTH
COL_Q = 2 * SGU_WIDTH
COL_KC = COL_Q + NSA_WIDTH
COL_VC = COL_KC + KV_WIDTH
COL_KS = COL_VC + KV_WIDTH
COL_VS = COL_KS + KV_WIDTH
COL_KW = COL_VS + KV_WIDTH
COL_VW = COL_KW + KV_WIDTH
COL_GT = COL_VW + KV_WIDTH

LANES = 128
VMEM_LIMIT = 56 * 1024 * 1024
ROW_TILE = 512
COL_TILE = 512


def _params(sem, vmem=VMEM_LIMIT):
    return pltpu.CompilerParams(dimension_semantics=sem, vmem_limit_bytes=vmem)


def _rms_scale(x, gain_row):
    ms = jnp.mean(x * x, axis=-1, keepdims=True)
    return x * lax.rsqrt(ms + EPS) * gain_row


def _masked_softmax(s, mask):
    s = jnp.where(mask, s, NEG)
    m = jnp.max(s, axis=-1, keepdims=True)
    p = jnp.exp(s - m) * mask.astype(F32)
    denom = jnp.maximum(jnp.sum(p, axis=-1, keepdims=True), 1e-30)
    return p * (1.0 / denom)


def _dot_nt(a, b):
    return lax.dot_general(a, b, (((1,), (1,)), ((), ())), preferred_element_type=F32)


def _norm_matmul_kernel(x_ref, g_ref, w_ref, ws_ref, o_ref, os_ref, xn_ref):
    @pl.when(pl.program_id(1) == 0)
    def _():
        xn_ref[...] = _rms_scale(x_ref[...], g_ref[...]).astype(BF16)
        os_ref[...] = jnp.dot(xn_ref[...], ws_ref[...], preferred_element_type=F32)

    o_ref[...] = jnp.dot(xn_ref[...], w_ref[...], preferred_element_type=F32).astype(o_ref.dtype)


def _norm_matmul(x, gain, w, w_side, tm, tn):
    m, d = x.shape
    n = w.shape[1]
    ns = w_side.shape[1]
    return pl.pallas_call(
        _norm_matmul_kernel,
        grid=(m // tm, n // tn),
        in_specs=[
            pl.BlockSpec((tm, d), lambda i, j: (i, 0)),
            pl.BlockSpec((1, d), lambda i, j: (0, 0)),
            pl.BlockSpec((d, tn), lambda i, j: (0, j)),
            pl.BlockSpec((d, ns), lambda i, j: (0, 0)),
        ],
        out_specs=[
            pl.BlockSpec((tm, tn), lambda i, j: (i, j)),
            pl.BlockSpec((tm, ns), lambda i, j: (i, 0)),
        ],
        out_shape=[jax.ShapeDtypeStruct((m, n), BF16), jax.ShapeDtypeStruct((m, ns), F32)],
        scratch_shapes=[pltpu.VMEM((tm, d), BF16)],
        compiler_params=_params(("arbitrary", "arbitrary")),
        name="norm_matmul",
    )(x, gain.reshape(1, d), w, w_side)


def _sgu_kernel(u_ref, v_ref, w_ref, bt_ref, gain_ref, o_ref, *, chunks):
    row = lax.broadcasted_iota(jnp.int32, (SGU_CHUNK, SGU_CHUNK), 0)
    col = lax.broadcasted_iota(jnp.int32, (SGU_CHUNK, SGU_CHUNK), 1)
    tri = row >= col
    for g in range(SGU_GROUPS):
        w = jnp.where(tri, w_ref[g], 0.0).astype(BF16)
        bias = bt_ref[:, g:g + 1]
        gain = gain_ref[g:g + 1, :]
        cs = slice(g * HEAD_DIM, (g + 1) * HEAD_DIM)
        for c in range(chunks):
            rs = slice(c * SGU_CHUNK, (c + 1) * SGU_CHUNK)
            u = jax.nn.gelu(u_ref[rs, cs].astype(F32))
            v = _rms_scale(jax.nn.gelu(v_ref[rs, cs].astype(F32)), gain)
            sg = jnp.dot(w, v.astype(BF16), preferred_element_type=F32) + bias
            o_ref[rs, cs] = (u * sg).astype(o_ref.dtype)


def _sgu(proj, sgu_w, sgu_b, sgu_g, tr):
    m = proj.shape[0]
    chunks = tr // SGU_CHUNK
    return pl.pallas_call(
        functools.partial(_sgu_kernel, chunks=chunks),
        grid=(m // tr,),
        in_specs=[
            pl.BlockSpec((tr, SGU_WIDTH), lambda i: (i, COL_U // SGU_WIDTH)),
            pl.BlockSpec((tr, SGU_WIDTH), lambda i: (i, COL_V // SGU_WIDTH)),
            pl.BlockSpec((SGU_GROUPS, SGU_CHUNK, SGU_CHUNK), lambda i: (0, 0, 0)),
            pl.BlockSpec((SGU_CHUNK, SGU_GROUPS), lambda i: (0, 0)),
            pl.BlockSpec((SGU_GROUPS, HEAD_DIM), lambda i: (0, 0)),
        ],
        out_specs=pl.BlockSpec((tr, SGU_WIDTH), lambda i: (i, 0)),
        out_shape=jax.ShapeDtypeStruct((m, SGU_WIDTH), BF16),
        compiler_params=_params(("arbitrary",)),
        name="sgu_mixer",
    )(proj, proj, sgu_w, sgu_b.T, sgu_g)


def _compress_kernel(x_ref, pe_ref, w1_ref, w2_ref, o_ref, xf_ref, *, seq):
    nc = seq // CMP_STRIDE
    xf_ref[0:seq, :] = x_ref[0].astype(F32)
    xf_ref[seq:seq + CMP_STRIDE, :] = jnp.zeros((CMP_STRIDE, HEAD_DIM), F32)
    acc = jnp.zeros((nc, HEAD_DIM), F32)
    for l in range(CMP_LEN):
        rows = xf_ref[pl.ds(l, nc, stride=CMP_STRIDE), :] + pe_ref[0, l:l + 1, :]
        acc = acc + jnp.dot(rows.astype(BF16), w1_ref[0, l], preferred_element_type=F32)
    hid = jax.nn.gelu(acc)
    o_ref[0, 0] = jnp.dot(hid.astype(BF16), w2_ref[0], preferred_element_type=F32).astype(o_ref.dtype)


def _compress(proj3, pe, w1, w2):
    b, s, _ = proj3.shape
    nc = s // CMP_STRIDE
    g = NSA_KV_GROUPS
    col0 = COL_KC // HEAD_DIM
    return pl.pallas_call(
        functools.partial(_compress_kernel, seq=s),
        grid=(2, b, g),
        in_specs=[
            pl.BlockSpec((1, s, HEAD_DIM), lambda kv, bi, gi: (bi, 0, col0 + g * kv + gi)),
            pl.BlockSpec((1, CMP_LEN, HEAD_DIM), lambda kv, bi, gi: (kv, 0, 0)),
            pl.BlockSpec((1, CMP_LEN, HEAD_DIM, HEAD_DIM), lambda kv, bi, gi: (kv, 0, 0, 0)),
            pl.BlockSpec((1, HEAD_DIM, HEAD_DIM), lambda kv, bi, gi: (kv, 0, 0)),
        ],
        out_specs=pl.BlockSpec((1, 1, nc, HEAD_DIM), lambda kv, bi, gi: (kv, bi * g + gi, 0, 0)),
        out_shape=jax.ShapeDtypeStruct((2, b * g, nc, HEAD_DIM), BF16),
        scratch_shapes=[pltpu.VMEM((s + CMP_STRIDE, HEAD_DIM), F32)],
        compiler_params=_params(("arbitrary", "arbitrary", "arbitrary")),
        name="kv_compress",
    )(proj3, pe, w1, w2)


def _importance_matrix(nc, n_slc):
    r1 = SLC_LEN // CMP_STRIDE
    r2 = CMP_LEN // CMP_STRIDE
    mat = np.zeros((nc, n_slc), np.float32)
    for j in range(n_slc):
        for m_ in range(r1):
            for n_ in range(r2):
                n = r1 * j + m_ - n_
                if 0 <= n < nc:
                    mat[n, j] += 1.0
    return mat


CMP_BAND = 256


def _cmp_select_kernel(slopes_ref, q_ref, kc_ref, vc_ref, m_ref, oc_ref, idx_ref, imp_ref, *, nc, n_slc, top_n):
    g = pl.program_id(1)
    q0 = pl.program_id(2) * Q_BLOCK
    t = q0 + lax.broadcasted_iota(jnp.int32, (Q_BLOCK, 1), 0)
    band_cols = min(CMP_BAND, nc)
    n_bands = nc // band_cols
    band = ((q0 + Q_BLOCK) // CMP_STRIDE - 1) // band_cols

    def attend(cols):
        n_ix = lax.broadcasted_iota(jnp.int32, (1, cols), 1)
        valid = (n_ix * CMP_STRIDE + (CMP_LEN - 1)) <= t
        dist = t.astype(F32) - ((n_ix * CMP_STRIDE).astype(F32) + (CMP_LEN - 1) / 2)
        kc = kc_ref[0, 0, :cols, :]
        vc = vc_ref[0, 0, :cols, :]
        psum = jnp.zeros((Q_BLOCK, cols), F32)
        for h in range(NSA_REP):
            cs = slice(h * HEAD_DIM, (h + 1) * HEAD_DIM)
            slope = slopes_ref[g * NSA_REP + h]
            s = _dot_nt(q_ref[0, :, cs], kc) * ATTN_SCALE - slope * dist
            p = _masked_softmax(s, valid)
            psum = psum + p
            oc_ref[0, :, cs] = jnp.dot(p.astype(BF16), vc, preferred_element_type=F32).astype(oc_ref.dtype)
        hi = psum.astype(BF16)
        lo = (psum - hi.astype(F32)).astype(BF16)
        mat = m_ref[:cols, :]
        imp_ref[...] = jnp.dot(hi, mat, preferred_element_type=F32) + jnp.dot(lo, mat, preferred_element_type=F32)

    for k in range(n_bands):
        pl.when(band == k)(functools.partial(attend, (k + 1) * band_cols))

    imp_t = imp_ref[...].T
    t_row = q0 + lax.broadcasted_iota(jnp.int32, (1, Q_BLOCK), 1)
    j_ix = lax.broadcasted_iota(jnp.int32, (n_slc, 1), 0)
    cur = t_row // SLC_LEN
    causal = j_ix <= cur
    forced = (j_ix == 0) | (j_ix == cur) | (j_ix == cur - 1)
    score = jnp.where(forced, FORCE_SCORE, jnp.where(causal, imp_t, -1.0))
    j_f = j_ix.astype(F32)
    r_ix = lax.broadcasted_iota(jnp.int32, (top_n, 1), 0)
    idx = jnp.zeros((top_n, Q_BLOCK), jnp.int32)
    for r in range(top_n):
        best = jnp.max(score, axis=0, keepdims=True)
        pick = jnp.min(jnp.where(score == best, j_f, float(n_slc)), axis=0, keepdims=True)
        score = jnp.where(j_f == pick, REMOVED, score)
        idx = jnp.where(r_ix == r, pick.astype(jnp.int32), idx)
    idx_ref[0, 0] = idx


def _cmp_select(proj3, cmp_kv, slopes, imp_mat, top_n):
    b, s, _ = proj3.shape
    g = NSA_KV_GROUPS
    nc = s // CMP_STRIDE
    n_slc = s // SLC_LEN
    grid_spec = pltpu.PrefetchScalarGridSpec(
        num_scalar_prefetch=1,
        grid=(b, g, s // Q_BLOCK),
        in_specs=[
            pl.BlockSpec((1, Q_BLOCK, GROUP_Q_WIDTH), lambda bi, gi, qi, sl: (bi, qi, COL_Q // GROUP_Q_WIDTH + gi)),
            pl.BlockSpec((1, 1, nc, HEAD_DIM), lambda bi, gi, qi, sl: (0, bi * g + gi, 0, 0)),
            pl.BlockSpec((1, 1, nc, HEAD_DIM), lambda bi, gi, qi, sl: (1, bi * g + gi, 0, 0)),
            pl.BlockSpec((nc, n_slc), lambda bi, gi, qi, sl: (0, 0)),
        ],
        out_specs=[
            pl.BlockSpec((1, Q_BLOCK, GROUP_Q_WIDTH), lambda bi, gi, qi, sl: (bi, qi, gi)),
            pl.BlockSpec((1, 1, top_n, Q_BLOCK), lambda bi, gi, qi, sl: (bi, gi, 0, qi)),
        ],
        scratch_shapes=[pltpu.VMEM((Q_BLOCK, n_slc), F32)],
    )
    return pl.pallas_call(
        functools.partial(_cmp_select_kernel, nc=nc, n_slc=n_slc, top_n=top_n),
        grid_spec=grid_spec,
        out_shape=[
            jax.ShapeDtypeStruct((b, s, NSA_WIDTH), F32),
            jax.ShapeDtypeStruct((b, g, top_n, s), jnp.int32),
        ],
        compiler_params=_params(("arbitrary", "arbitrary", "arbitrary")),
        name="cmp_attn_select",
    )(slopes, proj3, cmp_kv, cmp_kv, imp_mat)


KT_ROWS = HEAD_DIM + 16
POS_HI = float(LANES)
Q_ROWS = 8
GRP = 4
SLC_Q_TILE = 256
SLC_UNROLL = 6


def _slc_kernel(slopes_ref, idx_ref, q_ref, k_ref, v_ref, o_ref, qs_ref, os_ref, kt_ref, s_buf, e_buf, *, top_n, seq, qb):
    g = pl.program_id(1)
    q0 = pl.program_id(2) * qb
    n_tiles = top_n * SLC_LEN // LANES
    n_grp = qb // GRP

    @pl.when(pl.program_id(2) == 0)
    def _():
        sub = lax.broadcasted_iota(jnp.int32, (KT_ROWS - HEAD_DIM, LANES), 0)
        lane_f = lax.broadcasted_iota(jnp.int32, (KT_ROWS - HEAD_DIM, LANES), 1).astype(F32)

        def tr_body(i, carry):
            r0 = pl.multiple_of(i * LANES, LANES)
            tile = k_ref[0, pl.ds(r0, LANES), :].astype(F32).T
            pos_rows = jnp.where(sub == 0, lax.convert_element_type(i, F32), jnp.where(sub == 1, lane_f, 0.0))
            ext = jnp.concatenate([tile, pos_rows], axis=0)
            kt_ref[2 * i] = ext.astype(BF16)
            kt_ref[2 * i + 1] = pltpu.roll(ext, SLC_LEN, 1).astype(BF16)
            return carry
        lax.fori_loop(0, seq // LANES, tr_body, 0)
        s_buf[...] = jnp.zeros_like(s_buf)
        e_buf[...] = jnp.zeros_like(e_buf)
        qs_ref[...] = jnp.zeros_like(qs_ref)

    for h in range(NSA_REP):
        qs_ref[pl.ds(h, qb, stride=Q_ROWS), :] = q_ref[0, :, h * HEAD_DIM:(h + 1) * HEAD_DIM].astype(F32)
    row = lax.broadcasted_iota(jnp.int32, (Q_ROWS, 1), 0)
    low_rows = row < NSA_REP
    e_row = lax.broadcasted_iota(jnp.int32, (Q_ROWS, KT_ROWS - HEAD_DIM), 0)
    e_col = lax.broadcasted_iota(jnp.int32, (Q_ROWS, KT_ROWS - HEAD_DIM), 1)
    slope_rows = []
    pos_cols = []
    pos_row = [NSA_REP, 0]
    for par in range(2):
        sl = jnp.zeros((Q_ROWS, 1), F32)
        for h in range(NSA_REP):
            sl = jnp.where(row == h + par * NSA_REP, slopes_ref[g * NSA_REP + h], sl)
        slope_rows.append(sl)
        pos_cols.append(jnp.where((e_row == pos_row[par]) & (e_col == 0), POS_HI,
                                  jnp.where((e_row == pos_row[par]) & (e_col == 1), 1.0, 0.0)).astype(BF16))
    low_half = lax.broadcasted_iota(jnp.int32, (KT_ROWS, LANES), 1) < SLC_LEN

    def block_ids(tloc):
        return [idx_ref[0, 0, 0, 0, tloc * top_n + r] for r in range(top_n)]

    def stage_scores(grp, slot):
        for qi in range(GRP):
            par = qi % 2
            tloc = grp * GRP + qi
            ids = block_ids(tloc)
            q_rows = qs_ref[pl.ds(pl.multiple_of(tloc * Q_ROWS, Q_ROWS), Q_ROWS), :]
            if par:
                q_rows = pltpu.roll(q_rows, NSA_REP, 0)
            lhs = jnp.concatenate([q_rows.astype(BF16), pos_cols[par]], axis=1)
            tiles = [jnp.where(low_half, kt_ref[ids[2 * c]], kt_ref[ids[2 * c + 1] ^ 1]) for c in range(n_tiles)]
            s_buf[slot, qi] = jnp.dot(lhs, jnp.concatenate(tiles, axis=1), preferred_element_type=F32)

    def stage_softmax(grp, slot):
        for qi in range(GRP):
            par = qi % 2
            t_f = lax.convert_element_type(q0 + grp * GRP + qi, F32)
            raw = s_buf[slot, qi]
            key_pos = raw[pos_row[par]:pos_row[par] + 1, :]
            mask = key_pos <= t_f
            s = raw * ATTN_SCALE - slope_rows[par] * (t_f - key_pos)
            e_buf[slot, qi] = _masked_softmax(s, mask)

    def stage_values(grp, slot):
        for w in range(GRP // 2):
            tloc = grp * GRP + 2 * w
            v_pair = []
            for par in range(2):
                v_pair.append(jnp.concatenate([v_ref[0, j] for j in block_ids(tloc + par)], axis=0))
            prob = jnp.where(low_rows, e_buf[slot, 2 * w], e_buf[slot, 2 * w + 1]).astype(BF16)
            acc = jnp.dot(prob, jnp.concatenate(v_pair, axis=1), preferred_element_type=F32)
            os_ref[pl.ds(pl.multiple_of(tloc * NSA_REP, 2 * NSA_REP), 2 * NSA_REP), :] = jnp.where(
                low_rows, acc[:, :HEAD_DIM], acc[:, HEAD_DIM:])

    def trip(it, cur):
        stage_values(jnp.clip(it - 2, 0, n_grp - 1), cur)
        stage_softmax(jnp.clip(it - 1, 0, n_grp - 1), 1 - cur)
        stage_scores(jnp.minimum(it, n_grp - 1), cur)

    trips = n_grp + 2
    unroll = math.gcd(trips, SLC_UNROLL)
    assert unroll % 2 == 0

    def body(outer, carry):
        for k in range(unroll):
            trip(outer * unroll + k, k % 2)
        return carry

    lax.fori_loop(0, trips // unroll, body, 0)
    for h in range(NSA_REP):
        o_ref[0, :, h * HEAD_DIM:(h + 1) * HEAD_DIM] = os_ref[pl.ds(h, qb, stride=NSA_REP), :].astype(o_ref.dtype)


def _slc_attention(proj3, idx, slopes, top_n):
    b, s, width = proj3.shape
    g = NSA_KV_GROUPS
    n_slc = s // SLC_LEN
    qb = _pick_tile(s, SLC_Q_TILE)
    assert top_n % (LANES // SLC_LEN) == 0 and qb % GRP == 0 and GRP % 2 == 0
    proj4 = proj3.reshape(b, n_slc, SLC_LEN, width)
    idx_q = idx.reshape(b, g, top_n, s // qb, qb).transpose(0, 1, 3, 4, 2).reshape(b, g, s // qb, 1, qb * top_n)
    grid_spec = pltpu.PrefetchScalarGridSpec(
        num_scalar_prefetch=1,
        grid=(b, g, s // qb),
        in_specs=[
            pl.BlockSpec((1, 1, 1, 1, qb * top_n), lambda bi, gi, qi, sl: (bi, gi, qi, 0, 0), memory_space=pltpu.SMEM),
            pl.BlockSpec((1, qb, GROUP_Q_WIDTH), lambda bi, gi, qi, sl: (bi, qi, COL_Q // GROUP_Q_WIDTH + gi)),
            pl.BlockSpec((1, s, HEAD_DIM), lambda bi, gi, qi, sl: (bi, 0, COL_KS // HEAD_DIM + gi)),
            pl.BlockSpec((1, n_slc, SLC_LEN, HEAD_DIM), lambda bi, gi, qi, sl: (bi, 0, 0, COL_VS // HEAD_DIM + gi)),
        ],
        out_specs=pl.BlockSpec((1, qb, GROUP_Q_WIDTH), lambda bi, gi, qi, sl: (bi, qi, gi)),
        scratch_shapes=[
            pltpu.VMEM((qb * Q_ROWS, HEAD_DIM), F32),
            pltpu.VMEM((qb * NSA_REP, HEAD_DIM), F32),
            pltpu.VMEM((n_slc, KT_ROWS, LANES), BF16),
            pltpu.VMEM((2, GRP, Q_ROWS, top_n * SLC_LEN), F32),
            pltpu.VMEM((2, GRP, Q_ROWS, top_n * SLC_LEN), F32),
        ],
    )
    return pl.pallas_call(
        functools.partial(_slc_kernel, top_n=top_n, seq=s, qb=qb),
        grid_spec=grid_spec,
        out_shape=jax.ShapeDtypeStruct((b, s, NSA_WIDTH), F32),
        compiler_params=_params(("arbitrary", "arbitrary", "arbitrary")),
        name="slc_attn",
    )(slopes, idx_q, proj3, proj3, proj4)


def _win_kernel(slopes_ref, q_ref, k_ref, v_ref, oc_ref, os_ref, gt_ref, y_ref, *, span):
    g = pl.program_id(1)
    q0 = pl.program_id(2) * Q_BLOCK
    start = pl.multiple_of(jnp.maximum(q0 - WINDOW, 0), Q_BLOCK)
    kw = k_ref[0, pl.ds(start, span), :]
    vw = v_ref[0, pl.ds(start, span), :]
    t = q0 + lax.broadcasted_iota(jnp.int32, (Q_BLOCK, 1), 0)
    s_pos = start + lax.broadcasted_iota(jnp.int32, (1, span), 1)
    rel = t - s_pos
    mask = (rel >= 0) & (rel < WINDOW)
    rel_f = rel.astype(F32)
    gates = jax.nn.sigmoid(gt_ref[...])
    for h in range(NSA_REP):
        cs = slice(h * HEAD_DIM, (h + 1) * HEAD_DIM)
        slope = slopes_ref[g * NSA_REP + h]
        s = _dot_nt(q_ref[0, :, cs], kw) * ATTN_SCALE - slope * rel_f
        p = _masked_softmax(s, mask)
        ow = jnp.dot(p.astype(BF16), vw, preferred_element_type=F32)
        c = h * N_BRANCH
        y = gates[:, c:c + 1] * oc_ref[0, :, cs] + gates[:, c + 1:c + 2] * os_ref[0, :, cs] + gates[:, c + 2:c + 3] * ow
        y_ref[0, :, cs] = y.astype(y_ref.dtype)


def _win_combine(proj3, o_cmp, o_slc, gates_raw, slopes):
    b, s, _ = proj3.shape
    g = NSA_KV_GROUPS
    nq = s // Q_BLOCK
    span = min(WINDOW + Q_BLOCK, s)
    grid_spec = pltpu.PrefetchScalarGridSpec(
        num_scalar_prefetch=1,
        grid=(b, g, nq),
        in_specs=[
            pl.BlockSpec((1, Q_BLOCK, GROUP_Q_WIDTH), lambda bi, gi, qi, sl: (bi, qi, COL_Q // GROUP_Q_WIDTH + gi)),
            pl.BlockSpec((1, s, HEAD_DIM), lambda bi, gi, qi, sl: (bi, 0, COL_KW // HEAD_DIM + gi)),
            pl.BlockSpec((1, s, HEAD_DIM), lambda bi, gi, qi, sl: (bi, 0, COL_VW // HEAD_DIM + gi)),
            pl.BlockSpec((1, Q_BLOCK, GROUP_Q_WIDTH), lambda bi, gi, qi, sl: (bi, qi, gi)),
            pl.BlockSpec((1, Q_BLOCK, GROUP_Q_WIDTH), lambda bi, gi, qi, sl: (bi, qi, gi)),
            pl.BlockSpec((Q_BLOCK, LANES), lambda bi, gi, qi, sl: (bi * nq + qi, gi)),
        ],
        out_specs=pl.BlockSpec((1, Q_BLOCK, GROUP_Q_WIDTH), lambda bi, gi, qi, sl: (bi, qi, gi)),
    )
    return pl.pallas_call(
        functools.partial(_win_kernel, span=span),
        grid_spec=grid_spec,
        out_shape=jax.ShapeDtypeStruct((b, s, NSA_WIDTH), BF16),
        compiler_params=_params(("arbitrary", "arbitrary", "arbitrary")),
        name="win_attn_combine",
    )(slopes, proj3, proj3, proj3, o_cmp, o_slc, gates_raw)


def _matmul_res_kernel(*refs, n_pairs):
    r_ref = refs[2 * n_pairs]
    o_ref = refs[2 * n_pairs + 1]
    acc = r_ref[...]
    for i in range(n_pairs):
        acc = acc + jnp.dot(refs[i][...], refs[n_pairs + i][...], preferred_element_type=F32)
    o_ref[...] = acc


def _matmul_res(acts, weights, resid, tm, tn):
    m, n = resid.shape
    n_pairs = len(acts)
    in_specs = [pl.BlockSpec((tm, a.shape[1]), lambda i, j: (i, 0)) for a in acts]
    in_specs += [pl.BlockSpec((w.shape[0], tn), lambda i, j: (0, j)) for w in weights]
    in_specs += [pl.BlockSpec((tm, tn), lambda i, j: (i, j))]
    return pl.pallas_call(
        functools.partial(_matmul_res_kernel, n_pairs=n_pairs),
        grid=(m // tm, n // tn),
        in_specs=in_specs,
        out_specs=pl.BlockSpec((tm, tn), lambda i, j: (i, j)),
        out_shape=jax.ShapeDtypeStruct((m, n), F32),
        compiler_params=_params(("arbitrary", "arbitrary")),
        name="matmul_residual",
    )(*acts, *weights, resid)


def _ffn_kernel(h_ref, g_ref, wg_ref, wu_ref, wd_ref, gf_ref, o_ref, xn_ref, acc_ref, *, final_norm):
    k = pl.program_id(1)

    @pl.when(k == 0)
    def _():
        xn_ref[...] = _rms_scale(h_ref[...], g_ref[...]).astype(BF16)
        acc_ref[...] = jnp.zeros_like(acc_ref)

    xn = xn_ref[...]
    a = jnp.dot(xn, wg_ref[...], preferred_element_type=F32)
    u = jnp.dot(xn, wu_ref[...], preferred_element_type=F32)
    act = (a * jax.nn.sigmoid(a) * u).astype(BF16)
    acc_ref[...] += jnp.dot(act, wd_ref[...], preferred_element_type=F32)

    @pl.when(k == pl.num_programs(1) - 1)
    def _():
        out = h_ref[...] + acc_ref[...]
        if final_norm:
            out = _rms_scale(out, gf_ref[...])
        o_ref[...] = out


def _ffn(h, gain, w_gate, w_up, w_down, gain_final, final_norm, tm, tk):
    m, d = h.shape
    dff = w_gate.shape[1]
    return pl.pallas_call(
        functools.partial(_ffn_kernel, final_norm=final_norm),
        grid=(m // tm, dff // tk),
        in_specs=[
            pl.BlockSpec((tm, d), lambda i, k: (i, 0)),
            pl.BlockSpec((1, d), lambda i, k: (0, 0)),
            pl.BlockSpec((d, tk), lambda i, k: (0, k)),
            pl.BlockSpec((d, tk), lambda i, k: (0, k)),
            pl.BlockSpec((tk, d), lambda i, k: (k, 0)),
            pl.BlockSpec((1, d), lambda i, k: (0, 0)),
        ],
        out_specs=pl.BlockSpec((tm, d), lambda i, k: (i, 0)),
        out_shape=jax.ShapeDtypeStruct((m, d), F32),
        scratch_shapes=[pltpu.VMEM((tm, d), BF16), pltpu.VMEM((tm, d), F32)],
        compiler_params=_params(("arbitrary", "arbitrary")),
        name="ffn_swiglu",
    )(h, gain.reshape(1, d), w_gate, w_up, w_down, gain_final.reshape(1, d))


TAIL_ROWS = 8


def _conv_in_kernel(x_ref, g_ref, wb_ref, wc_ref, wz_ref, cw_ref, y_ref, xn_ref, tail_ref, *, tiles_per_seq):
    i = pl.program_id(0)
    j = pl.program_id(1)

    @pl.when(j == 0)
    def _():
        xn_ref[...] = _rms_scale(x_ref[...], g_ref[...]).astype(BF16)

    xn = xn_ref[...]
    bg = jnp.dot(xn, wb_ref[...], preferred_element_type=F32)
    z = jnp.dot(xn, wc_ref[...], preferred_element_type=F32) * jnp.dot(xn, wz_ref[...], preferred_element_type=F32)
    tm = z.shape[0]
    seq_start = (i % tiles_per_seq) == 0
    prev = jnp.where(seq_start, 0.0, tail_ref[j])
    tail_ref[j] = z[tm - TAIL_ROWS:, :]
    prev1 = prev[TAIL_ROWS - 1:TAIL_ROWS, :]
    prev2 = prev[TAIL_ROWS - 2:TAIL_ROWS - 1, :]
    row = lax.broadcasted_iota(jnp.int32, (tm, 1), 0)
    z1 = jnp.where(row >= 1, pltpu.roll(z, 1, 0), prev1)
    z2 = jnp.where(row >= 2, pltpu.roll(z, 2, 0), jnp.where(row == 1, prev1, prev2))
    zc = cw_ref[2:3, :] * z + cw_ref[1:2, :] * z1 + cw_ref[0:1, :] * z2
    y_ref[...] = (bg * zc).astype(y_ref.dtype)


def _conv_in(h, gain, w_in, conv_w, seq, tm, tn):
    m, d = h.shape
    cw = conv_w.shape[1]
    nj = cw // tn
    return pl.pallas_call(
        functools.partial(_conv_in_kernel, tiles_per_seq=seq // tm),
        grid=(m // tm, nj),
        in_specs=[
            pl.BlockSpec((tm, d), lambda i, j: (i, 0)),
            pl.BlockSpec((1, d), lambda i, j: (0, 0)),
            pl.BlockSpec((d, tn), lambda i, j: (0, j)),
            pl.BlockSpec((d, tn), lambda i, j: (0, nj + j)),
            pl.BlockSpec((d, tn), lambda i, j: (0, 2 * nj + j)),
            pl.BlockSpec((conv_w.shape[0], tn), lambda i, j: (0, j)),
        ],
        out_specs=pl.BlockSpec((tm, tn), lambda i, j: (i, j)),
        out_shape=jax.ShapeDtypeStruct((m, cw), BF16),
        scratch_shapes=[pltpu.VMEM((tm, d), BF16), pltpu.VMEM((nj, TAIL_ROWS, tn), F32)],
        compiler_params=_params(("arbitrary", "arbitrary")),
        name="conv_mixer_in",
    )(h, gain.reshape(1, d), w_in, w_in, w_in, conv_w)


def _gate_weight(w_in):
    per_group = NSA_REP * N_BRANCH
    cols = []
    for g in range(NSA_KV_GROUPS):
        blk = w_in[:, COL_GT + g * per_group:COL_GT + (g + 1) * per_group]
        cols.append(jnp.pad(blk, ((0, 0), (0, LANES - per_group))))
    return jnp.concatenate(cols, axis=1)


def _pick_tile(m, pref):
    t = min(m, pref)
    while m % t:
        t //= 2
    return t


def kernel(x, norm_mix, norm_ffn, norm_f, w_in_ab, w_out_ab, sgu_w, sgu_b, sgu_g, cmp_pe_k, cmp_w1_k, cmp_w2_k, cmp_pe_v, cmp_w1_v, cmp_w2_v, w_in_c, conv_w, w_out_c, w_gate, w_up, w_down):
    b, s, d = x.shape
    m = b * s
    tm = _pick_tile(m, ROW_TILE)
    tm_wide = _pick_tile(s, 2 * ROW_TILE)
    top_n = min(SLC_TOP_N, s // SLC_LEN)
    slopes = (2.0 ** (-8.0 * jnp.arange(1, NSA_HEADS + 1, dtype=F32) / NSA_HEADS)).astype(F32)
    imp_mat = jnp.asarray(_importance_matrix(s // CMP_STRIDE, s // SLC_LEN), BF16)

    h = x.reshape(m, d)

    w_in = w_in_ab[0]
    proj, gates_raw = _norm_matmul(h, norm_mix[0], w_in[:, :COL_GT].astype(BF16), _gate_weight(w_in).astype(BF16),
                                   tm_wide, COL_GT // 3)
    y_a = _sgu(proj, sgu_w[0], sgu_b[0], sgu_g[0], tm)
    proj3 = proj.reshape(b, s, COL_GT)
    cmp_kv = _compress(
        proj3,
        jnp.stack([cmp_pe_k[0], cmp_pe_v[0]]),
        jnp.stack([cmp_w1_k[0], cmp_w1_v[0]]).astype(BF16),
        jnp.stack([cmp_w2_k[0], cmp_w2_v[0]]).astype(BF16),
    )
    o_cmp, idx = _cmp_select(proj3, cmp_kv, slopes, imp_mat, top_n)
    o_slc = _slc_attention(proj3, idx, slopes, top_n)
    y_b = _win_combine(proj3, o_cmp, o_slc, gates_raw, slopes)
    w_out = w_out_ab[0].astype(BF16)
    h = _matmul_res([y_a, y_b.reshape(m, NSA_WIDTH)], [w_out[:SGU_WIDTH], w_out[SGU_WIDTH:]], h, tm, d)
    h = _ffn(h, norm_ffn[0], w_gate[0].astype(BF16), w_up[0].astype(BF16), w_down[0].astype(BF16), norm_f, False, tm, COL_TILE)

    y_c = _conv_in(h, norm_mix[1], w_in_c[0].astype(BF16), conv_w[0], s, tm_wide, COL_TILE)
    h = _matmul_res([y_c], [w_out_c[0].astype(BF16)], h, tm, d)
    h = _ffn(h, norm_ffn[1], w_gate[1].astype(BF16), w_up[1].astype(BF16), w_down[1].astype(BF16), norm_f, True, tm, COL_TILE)
    return h.reshape(b, s, d)
```

```python
import functools
import math

import jax
import jax.numpy as jnp
import numpy as np
from jax import lax
from jax.experimental import pallas as pl
from jax.experimental.pallas import tpu as pltpu

F32 = jnp.float32
BF16 = jnp.bfloat16

HEAD_DIM = 128
SGU_GROUPS = 8
SGU_CHUNK = 128
SGU_WIDTH = SGU_GROUPS * HEAD_DIM
NSA_HEADS = 8
NSA_KV_GROUPS = 2
NSA_REP = NSA_HEADS // NSA_KV_GROUPS
NSA_WIDTH = NSA_HEADS * HEAD_DIM
KV_WIDTH = NSA_KV_GROUPS * HEAD_DIM
GROUP_Q_WIDTH = NSA_REP * HEAD_DIM
CMP_LEN = 32
CMP_STRIDE = 16
SLC_LEN = 64
SLC_TOP_N = 16
WINDOW = 512
Q_BLOCK = 128
N_BRANCH = 3
FORCE_SCORE = 1e9
EPS = 1e-6
NEG = -1e30
REMOVED = -3e38
ATTN_SCALE = HEAD_DIM ** -0.5

COL_U = 0
COL_V = SGU_WIDTH
COL_Q = 2 * SGU_WIDTH
COL_KC = COL_Q + NSA_WIDTH
COL_VC = COL_KC + KV_WIDTH
COL_KS = COL_VC + KV_WIDTH
COL_VS = COL_KS + KV_WIDTH
COL_KW = COL_VS + KV_WIDTH
COL_VW = COL_KW + KV_WIDTH
COL_GT = COL_VW + KV_WIDTH

LANES = 128
VMEM_LIMIT = 56 * 1024 * 1024
ROW_TILE = 512
COL_TILE = 512


def _params(sem, vmem=VMEM_LIMIT):
    return pltpu.CompilerParams(dimension_semantics=sem, vmem_limit_bytes=vmem)


def _rms_scale(x, gain_row):
    ms = jnp.mean(x * x, axis=-1, keepdims=True)
    return x * lax.rsqrt(ms + EPS) * gain_row


def _masked_softmax(s, mask):
    s = jnp.where(mask, s, NEG)
    m = jnp.max(s, axis=-1, keepdims=True)
    p = jnp.exp(s - m) * mask.astype(F32)
    denom = jnp.maximum(jnp.sum(p, axis=-1, keepdims=True), 1e-30)
    return p * (1.0 / denom)


def _dot_nt(a, b):
    return lax.dot_general(a, b, (((1,), (1,)), ((), ())), preferred_element_type=F32)


def _norm_matmul_kernel(x_ref, g_ref, w_ref, ws_ref, o_ref, os_ref, xn_ref):
    @pl.when(pl.program_id(1) == 0)
    def _():
        xn_ref[...] = _rms_scale(x_ref[...], g_ref[...]).astype(BF16)
        os_ref[...] = jnp.dot(xn_ref[...], ws_ref[...], preferred_element_type=F32)

    o_ref[...] = jnp.dot(xn_ref[...], w_ref[...], preferred_element_type=F32).astype(o_ref.dtype)


def _norm_matmul(x, gain, w, w_side, tm, tn):
    m, d = x.shape
    n = w.shape[1]
    ns = w_side.shape[1]
    return pl.pallas_call(
        _norm_matmul_kernel,
        grid=(m // tm, n // tn),
        in_specs=[
            pl.BlockSpec((tm, d), lambda i, j: (i, 0)),
            pl.BlockSpec((1, d), lambda i, j: (0, 0)),
            pl.BlockSpec((d, tn), lambda i, j: (0, j)),
            pl.BlockSpec((d, ns), lambda i, j: (0, 0)),
        ],
        out_specs=[
            pl.BlockSpec((tm, tn), lambda i, j: (i, j)),
            pl.BlockSpec((tm, ns), lambda i, j: (i, 0)),
        ],
        out_shape=[jax.ShapeDtypeStruct((m, n), BF16), jax.ShapeDtypeStruct((m, ns), F32)],
        scratch_shapes=[pltpu.VMEM((tm, d), BF16)],
        compiler_params=_params(("arbitrary", "arbitrary")),
        name="norm_matmul",
    )(x, gain.reshape(1, d), w, w_side)


def _sgu_kernel(u_ref, v_ref, w_ref, bt_ref, gain_ref, o_ref, *, chunks):
    row = lax.broadcasted_iota(jnp.int32, (SGU_CHUNK, SGU_CHUNK), 0)
    col = lax.broadcasted_iota(jnp.int32, (SGU_CHUNK, SGU_CHUNK), 1)
    tri = row >= col
    for g in range(SGU_GROUPS):
        w = jnp.where(tri, w_ref[g], 0.0).astype(BF16)
        bias = bt_ref[:, g:g + 1]
        gain = gain_ref[g:g + 1, :]
        cs = slice(g * HEAD_DIM, (g + 1) * HEAD_DIM)
        for c in range(chunks):
            rs = slice(c * SGU_CHUNK, (c + 1) * SGU_CHUNK)
            u = jax.nn.gelu(u_ref[rs, cs].astype(F32))
            v = _rms_scale(jax.nn.gelu(v_ref[rs, cs].astype(F32)), gain)
            sg = jnp.dot(w, v.astype(BF16), preferred_element_type=F32) + bias
            o_ref[rs, cs] = (u * sg).astype(o_ref.dtype)


def _sgu(proj, sgu_w, sgu_b, sgu_g, tr):
    m = proj.shape[0]
    chunks = tr // SGU_CHUNK
    return pl.pallas_call(
        functools.partial(_sgu_kernel, chunks=chunks),
        grid=(m // tr,),
        in_specs=[
            pl.BlockSpec((tr, SGU_WIDTH), lambda i: (i, COL_U // SGU_WIDTH)),
            pl.BlockSpec((tr, SGU_WIDTH), lambda i: (i, COL_V // SGU_WIDTH)),
            pl.BlockSpec((SGU_GROUPS, SGU_CHUNK, SGU_CHUNK), lambda i: (0, 0, 0)),
            pl.BlockSpec((SGU_CHUNK, SGU_GROUPS), lambda i: (0, 0)),
            pl.BlockSpec((SGU_GROUPS, HEAD_DIM), lambda i: (0, 0)),
        ],
        out_specs=pl.BlockSpec((tr, SGU_WIDTH), lambda i: (i, 0)),
        out_shape=jax.ShapeDtypeStruct((m, SGU_WIDTH), BF16),
        compiler_params=_params(("arbitrary",)),
        name="sgu_mixer",
    )(proj, proj, sgu_w, sgu_b.T, sgu_g)


def _compress_kernel(x_ref, pe_ref, w1_ref, w2_ref, o_ref, xf_ref, *, seq):
    nc = seq // CMP_STRIDE
    xf_ref[0:seq, :] = x_ref[0].astype(F32)
    xf_ref[seq:seq + CMP_STRIDE, :] = jnp.zeros((CMP_STRIDE, HEAD_DIM), F32)
    acc = jnp.zeros((nc, HEAD_DIM), F32)
    for l in range(CMP_LEN):
        rows = xf_ref[pl.ds(l, nc, stride=CMP_STRIDE), :] + pe_ref[0, l:l + 1, :]
        acc = acc + jnp.dot(rows.astype(BF16), w1_ref[0, l], preferred_element_type=F32)
    hid = jax.nn.gelu(acc)
    o_ref[0, 0] = jnp.dot(hid.astype(BF16), w2_ref[0], preferred_element_type=F32).astype(o_ref.dtype)


def _compress(proj3, pe, w1, w2):
    b, s, _ = proj3.shape
    nc = s // CMP_STRIDE
    g = NSA_KV_GROUPS
    col0 = COL_KC // HEAD_DIM
    return pl.pallas_call(
        functools.partial(_compress_kernel, seq=s),
        grid=(2, b, g),
        in_specs=[
            pl.BlockSpec((1, s, HEAD_DIM), lambda kv, bi, gi: (bi, 0, col0 + g * kv + gi)),
            pl.BlockSpec((1, CMP_LEN, HEAD_DIM), lambda kv, bi, gi: (kv, 0, 0)),
            pl.BlockSpec((1, CMP_LEN, HEAD_DIM, HEAD_DIM), lambda kv, bi, gi: (kv, 0, 0, 0)),
            pl.BlockSpec((1, HEAD_DIM, HEAD_DIM), lambda kv, bi, gi: (kv, 0, 0)),
        ],
        out_specs=pl.BlockSpec((1, 1, nc, HEAD_DIM), lambda kv, bi, gi: (kv, bi * g + gi, 0, 0)),
        out_shape=jax.ShapeDtypeStruct((2, b * g, nc, HEAD_DIM), BF16),
        scratch_shapes=[pltpu.VMEM((s + CMP_STRIDE, HEAD_DIM), F32)],
        compiler_params=_params(("arbitrary", "arbitrary", "arbitrary")),
        name="kv_compress",
    )(proj3, pe, w1, w2)


def _importance_matrix(nc, n_slc):
    r1 = SLC_LEN // CMP_STRIDE
    r2 = CMP_LEN // CMP_STRIDE
    mat = np.zeros((nc, n_slc), np.float32)
    for j in range(n_slc):
        for m_ in range(r1):
            for n_ in range(r2):
                n = r1 * j + m_ - n_
                if 0 <= n < nc:
                    mat[n, j] += 1.0
    return mat


CMP_BAND = 256


def _cmp_select_kernel(slopes_ref, q_ref, kc_ref, vc_ref, m_ref, oc_ref, idx_ref, imp_ref, *, nc, n_slc, top_n):
    g = pl.program_id(1)
    q0 = pl.program_id(2) * Q_BLOCK
    t = q0 + lax.broadcasted_iota(jnp.int32, (Q_BLOCK, 1), 0)
    band_cols = min(CMP_BAND, nc)
    n_bands = nc // band_cols
    band = ((q0 + Q_BLOCK) // CMP_STRIDE - 1) // band_cols

    def attend(cols):
        n_ix = lax.broadcasted_iota(jnp.int32, (1, cols), 1)
        valid = (n_ix * CMP_STRIDE + (CMP_LEN - 1)) <= t
        dist = t.astype(F32) - ((n_ix * CMP_STRIDE).astype(F32) + (CMP_LEN - 1) / 2)
        kc = kc_ref[0, 0, :cols, :]
        vc = vc_ref[0, 0, :cols, :]
        psum = jnp.zeros((Q_BLOCK, cols), F32)
        for h in range(NSA_REP):
            cs = slice(h * HEAD_DIM, (h + 1) * HEAD_DIM)
            slope = slopes_ref[g * NSA_REP + h]
            s = _dot_nt(q_ref[0, :, cs], kc) * ATTN_SCALE - slope * dist
            p = _masked_softmax(s, valid)
            psum = psum + p
            oc_ref[0, :, cs] = jnp.dot(p.astype(BF16), vc, preferred_element_type=F32).astype(oc_ref.dtype)
        hi = psum.astype(BF16)
        lo = (psum - hi.astype(F32)).astype(BF16)
        mat = m_ref[:cols, :]
        imp_ref[...] = jnp.dot(hi, mat, preferred_element_type=F32) + jnp.dot(lo, mat, preferred_element_type=F32)

    for k in range(n_bands):
        pl.when(band == k)(functools.partial(attend, (k + 1) * band_cols))

    imp_t = imp_ref[...].T
    t_row = q0 + lax.broadcasted_iota(jnp.int32, (1, Q_BLOCK), 1)
    j_ix = lax.broadcasted_iota(jnp.int32, (n_slc, 1), 0)
    cur = t_row // SLC_LEN
    causal = j_ix <= cur
    forced = (j_ix == 0) | (j_ix == cur) | (j_ix == cur - 1)
    score = jnp.where(forced, FORCE_SCORE, jnp.where(causal, imp_t, -1.0))
    j_f = j_ix.astype(F32)
    r_ix = lax.broadcasted_iota(jnp.int32, (top_n, 1), 0)
    idx = jnp.zeros((top_n, Q_BLOCK), jnp.int32)
    for r in range(top_n):
        best = jnp.max(score, axis=0, keepdims=True)
        pick = jnp.min(jnp.where(score == best, j_f, float(n_slc)), axis=0, keepdims=True)
        score = jnp.where(j_f == pick, REMOVED, score)
        idx = jnp.where(r_ix == r, pick.astype(jnp.int32), idx)
    idx_ref[0, 0] = idx


def _cmp_select(proj3, cmp_kv, slopes, imp_mat, top_n):
    b, s, _ = proj3.shape
    g = NSA_KV_GROUPS
    nc = s // CMP_STRIDE
    n_slc = s // SLC_LEN
    grid_spec = pltpu.PrefetchScalarGridSpec(
        num_scalar_prefetch=1,
        grid=(b, g, s // Q_BLOCK),
        in_specs=[
            pl.BlockSpec((1, Q_BLOCK, GROUP_Q_WIDTH), lambda bi, gi, qi, sl: (bi, qi, COL_Q // GROUP_Q_WIDTH + gi)),
            pl.BlockSpec((1, 1, nc, HEAD_DIM), lambda bi, gi, qi, sl: (0, bi * g + gi, 0, 0)),
            pl.BlockSpec((1, 1, nc, HEAD_DIM), lambda bi, gi, qi, sl: (1, bi * g + gi, 0, 0)),
            pl.BlockSpec((nc, n_slc), lambda bi, gi, qi, sl: (0, 0)),
        ],
        out_specs=[
            pl.BlockSpec((1, Q_BLOCK, GROUP_Q_WIDTH), lambda bi, gi, qi, sl: (bi, qi, gi)),
            pl.BlockSpec((1, 1, top_n, Q_BLOCK), lambda bi, gi, qi, sl: (bi, gi, 0, qi)),
        ],
        scratch_shapes=[pltpu.VMEM((Q_BLOCK, n_slc), F32)],
    )
    return pl.pallas_call(
        functools.partial(_cmp_select_kernel, nc=nc, n_slc=n_slc, top_n=top_n),
        grid_spec=grid_spec,
        out_shape=[
            jax.ShapeDtypeStruct((b, s, NSA_WIDTH), F32),
            jax.ShapeDtypeStruct((b, g, top_n, s), jnp.int32),
        ],
        compiler_params=_params(("arbitrary", "arbitrary", "arbitrary")),
        name="cmp_attn_select",
    )(slopes, proj3, cmp_kv, cmp_kv, imp_mat)


KT_ROWS = HEAD_DIM + 16
POS_HI = float(LANES)
Q_ROWS = 8
GRP = 4
SLC_Q_TILE = 256
SLC_UNROLL = 22


def _slc_kernel(slopes_ref, idx_ref, q_ref, k_ref, v_ref, o_ref, qs_ref, os_ref, kt_ref, s_buf, e_buf, *, top_n, seq, qb):
    g = pl.program_id(1)
    q0 = pl.program_id(2) * qb
    n_tiles = top_n * SLC_LEN // LANES
    n_grp = qb // GRP

    @pl.when(pl.program_id(2) == 0)
    def _():
        sub = lax.broadcasted_iota(jnp.int32, (KT_ROWS - HEAD_DIM, LANES), 0)
        lane_f = lax.broadcasted_iota(jnp.int32, (KT_ROWS - HEAD_DIM, LANES), 1).astype(F32)

        def tr_body(i, carry):
            r0 = pl.multiple_of(i * LANES, LANES)
            tile = k_ref[0, pl.ds(r0, LANES), :].astype(F32).T
            pos_rows = jnp.where(sub == 0, lax.convert_element_type(i, F32), jnp.where(sub == 1, lane_f, 0.0))
            ext = jnp.concatenate([tile, pos_rows], axis=0)
            kt_ref[2 * i] = ext.astype(BF16)
            kt_ref[2 * i + 1] = pltpu.roll(ext, SLC_LEN, 1).astype(BF16)
            return carry
        lax.fori_loop(0, seq // LANES, tr_body, 0)
        s_buf[...] = jnp.zeros_like(s_buf)
        e_buf[...] = jnp.zeros_like(e_buf)
        qs_ref[...] = jnp.zeros_like(qs_ref)

    for h in range(NSA_REP):
        qs_ref[pl.ds(h, qb, stride=Q_ROWS), :] = q_ref[0, :, h * HEAD_DIM:(h + 1) * HEAD_DIM].astype(F32)
    row = lax.broadcasted_iota(jnp.int32, (Q_ROWS, 1), 0)
    low_rows = row < NSA_REP
    e_row = lax.broadcasted_iota(jnp.int32, (Q_ROWS, KT_ROWS - HEAD_DIM), 0)
    e_col = lax.broadcasted_iota(jnp.int32, (Q_ROWS, KT_ROWS - HEAD_DIM), 1)
    slope_rows = []
    pos_cols = []
    pos_row = [NSA_REP, 0]
    for par in range(2):
        sl = jnp.zeros((Q_ROWS, 1), F32)
        for h in range(NSA_REP):
            sl = jnp.where(row == h + par * NSA_REP, slopes_ref[g * NSA_REP + h], sl)
        slope_rows.append(sl)
        pos_cols.append(jnp.where((e_row == pos_row[par]) & (e_col == 0), POS_HI,
                                  jnp.where((e_row == pos_row[par]) & (e_col == 1), 1.0, 0.0)).astype(BF16))
    low_half = lax.broadcasted_iota(jnp.int32, (KT_ROWS, LANES), 1) < SLC_LEN

    def block_ids(tloc):
        return [idx_ref[0, 0, 0, 0, tloc * top_n + r] for r in range(top_n)]

    def stage_scores(grp, slot):
        for qi in range(GRP):
            par = qi % 2
            tloc = grp * GRP + qi
            ids = block_ids(tloc)
            q_rows = qs_ref[pl.ds(pl.multiple_of(tloc * Q_ROWS, Q_ROWS), Q_ROWS), :]
            if par:
                q_rows = pltpu.roll(q_rows, NSA_REP, 0)
            lhs = jnp.concatenate([q_rows.astype(BF16), pos_cols[par]], axis=1)
            tiles = [jnp.where(low_half, kt_ref[ids[2 * c]], kt_ref[ids[2 * c + 1] ^ 1]) for c in range(n_tiles)]
            s_buf[slot, qi] = jnp.dot(lhs, jnp.concatenate(tiles, axis=1), preferred_element_type=F32)

    def stage_softmax(grp, slot):
        for qi in range(GRP):
            par = qi % 2
            t_f = lax.convert_element_type(q0 + grp * GRP + qi, F32)
            raw = s_buf[slot, qi]
            key_pos = raw[pos_row[par]:pos_row[par] + 1, :]
            mask = key_pos <= t_f
            s = raw * ATTN_SCALE - slope_rows[par] * (t_f - key_pos)
            e_buf[slot, qi] = _masked_softmax(s, mask)

    def stage_values(grp, slot):
        for w in range(GRP // 2):
            tloc = grp * GRP + 2 * w
            v_pair = []
            for par in range(2):
                v_pair.append(jnp.concatenate([v_ref[0, j] for j in block_ids(tloc + par)], axis=0))
            prob = jnp.where(low_rows, e_buf[slot, 2 * w], e_buf[slot, 2 * w + 1]).astype(BF16)
            acc = jnp.dot(prob, jnp.concatenate(v_pair, axis=1), preferred_element_type=F32)
            os_ref[pl.ds(pl.multiple_of(tloc * NSA_REP, 2 * NSA_REP), 2 * NSA_REP), :] = jnp.where(
                low_rows, acc[:, :HEAD_DIM], acc[:, HEAD_DIM:])

    def trip(it, cur):
        stage_values(jnp.clip(it - 2, 0, n_grp - 1), cur)
        stage_softmax(jnp.clip(it - 1, 0, n_grp - 1), 1 - cur)
        stage_scores(jnp.minimum(it, n_grp - 1), cur)

    trips = n_grp + 2
    unroll = math.gcd(trips, SLC_UNROLL)
    assert unroll % 2 == 0

    def body(outer, carry):
        for k in range(unroll):
            trip(outer * unroll + k, k % 2)
        return carry

    lax.fori_loop(0, trips // unroll, body, 0)
    for h in range(NSA_REP):
        o_ref[0, :, h * HEAD_DIM:(h + 1) * HEAD_DIM] = os_ref[pl.ds(h, qb, stride=NSA_REP), :].astype(o_ref.dtype)


def _slc_attention(proj3, idx, slopes, top_n):
    b, s, width = proj3.shape
    g = NSA_KV_GROUPS
    n_slc = s // SLC_LEN
    qb = _pick_tile(s, SLC_Q_TILE)
    assert top_n % (LANES // SLC_LEN) == 0 and qb % GRP == 0 and GRP % 2 == 0
    proj4 = proj3.reshape(b, n_slc, SLC_LEN, width)
    idx_q = idx.reshape(b, g, top_n, s // qb, qb).transpose(0, 1, 3, 4, 2).reshape(b, g, s // qb, 1, qb * top_n)
    grid_spec = pltpu.PrefetchScalarGridSpec(
        num_scalar_prefetch=1,
        grid=(b, g, s // qb),
        in_specs=[
            pl.BlockSpec((1, 1, 1, 1, qb * top_n), lambda bi, gi, qi, sl: (bi, gi, qi, 0, 0), memory_space=pltpu.SMEM),
            pl.BlockSpec((1, qb, GROUP_Q_WIDTH), lambda bi, gi, qi, sl: (bi, qi, COL_Q // GROUP_Q_WIDTH + gi)),
            pl.BlockSpec((1, s, HEAD_DIM), lambda bi, gi, qi, sl: (bi, 0, COL_KS // HEAD_DIM + gi)),
            pl.BlockSpec((1, n_slc, SLC_LEN, HEAD_DIM), lambda bi, gi, qi, sl: (bi, 0, 0, COL_VS // HEAD_DIM + gi)),
        ],
        out_specs=pl.BlockSpec((1, qb, GROUP_Q_WIDTH), lambda bi, gi, qi, sl: (bi, qi, gi)),
        scratch_shapes=[
            pltpu.VMEM((qb * Q_ROWS, HEAD_DIM), F32),
            pltpu.VMEM((qb * NSA_REP, HEAD_DIM), F32),
            pltpu.VMEM((n_slc, KT_ROWS, LANES), BF16),
            pltpu.VMEM((2, GRP, Q_ROWS, top_n * SLC_LEN), F32),
            pltpu.VMEM((2, GRP, Q_ROWS, top_n * SLC_LEN), F32),
        ],
    )
    return pl.pallas_call(
        functools.partial(_slc_kernel, top_n=top_n, seq=s, qb=qb),
        grid_spec=grid_spec,
        out_shape=jax.ShapeDtypeStruct((b, s, NSA_WIDTH), F32),
        compiler_params=_params(("arbitrary", "arbitrary", "arbitrary")),
        name="slc_attn",
    )(slopes, idx_q, proj3, proj3, proj4)


def _win_kernel(slopes_ref, q_ref, k_ref, v_ref, oc_ref, os_ref, gt_ref, y_ref, *, span):
    g = pl.program_id(1)
    q0 = pl.program_id(2) * Q_BLOCK
    start = pl.multiple_of(jnp.maximum(q0 - WINDOW, 0), Q_BLOCK)
    kw = k_ref[0, pl.ds(start, span), :]
    vw = v_ref[0, pl.ds(start, span), :]
    t = q0 + lax.broadcasted_iota(jnp.int32, (Q_BLOCK, 1), 0)
    s_pos = start + lax.broadcasted_iota(jnp.int32, (1, span), 1)
    rel = t - s_pos
    mask = (rel >= 0) & (rel < WINDOW)
    rel_f = rel.astype(F32)
    gates = jax.nn.sigmoid(gt_ref[...])
    for h in range(NSA_REP):
        cs = slice(h * HEAD_DIM, (h + 1) * HEAD_DIM)
        slope = slopes_ref[g * NSA_REP + h]
        s = _dot_nt(q_ref[0, :, cs], kw) * ATTN_SCALE - slope * rel_f
        p = _masked_softmax(s, mask)
        ow = jnp.dot(p.astype(BF16), vw, preferred_element_type=F32)
        c = h * N_BRANCH
        y = gates[:, c:c + 1] * oc_ref[0, :, cs] + gates[:, c + 1:c + 2] * os_ref[0, :, cs] + gates[:, c + 2:c + 3] * ow
        y_ref[0, :, cs] = y.astype(y_ref.dtype)


def _win_combine(proj3, o_cmp, o_slc, gates_raw, slopes):
    b, s, _ = proj3.shape
    g = NSA_KV_GROUPS
    nq = s // Q_BLOCK
    span = min(WINDOW + Q_BLOCK, s)
    grid_spec = pltpu.PrefetchScalarGridSpec(
        num_scalar_prefetch=1,
        grid=(b, g, nq),
        in_specs=[
            pl.BlockSpec((1, Q_BLOCK, GROUP_Q_WIDTH), lambda bi, gi, qi, sl: (bi, qi, COL_Q // GROUP_Q_WIDTH + gi)),
            pl.BlockSpec((1, s, HEAD_DIM), lambda bi, gi, qi, sl: (bi, 0, COL_KW // HEAD_DIM + gi)),
            pl.BlockSpec((1, s, HEAD_DIM), lambda bi, gi, qi, sl: (bi, 0, COL_VW // HEAD_DIM + gi)),
            pl.BlockSpec((1, Q_BLOCK, GROUP_Q_WIDTH), lambda bi, gi, qi, sl: (bi, qi, gi)),
            pl.BlockSpec((1, Q_BLOCK, GROUP_Q_WIDTH), lambda bi, gi, qi, sl: (bi, qi, gi)),
            pl.BlockSpec((Q_BLOCK, LANES), lambda bi, gi, qi, sl: (bi * nq + qi, gi)),
        ],
        out_specs=pl.BlockSpec((1, Q_BLOCK, GROUP_Q_WIDTH), lambda bi, gi, qi, sl: (bi, qi, gi)),
    )
    return pl.pallas_call(
        functools.partial(_win_kernel, span=span),
        grid_spec=grid_spec,
        out_shape=jax.ShapeDtypeStruct((b, s, NSA_WIDTH), BF16),
        compiler_params=_params(("arbitrary", "arbitrary", "arbitrary")),
        name="win_attn_combine",
    )(slopes, proj3, proj3, proj3, o_cmp, o_slc, gates_raw)


def _matmul_res_kernel(*refs, n_pairs):
    r_ref = refs[2 * n_pairs]
    o_ref = refs[2 * n_pairs + 1]
    acc = r_ref[...]
    for i in range(n_pairs):
        acc = acc + jnp.dot(refs[i][...], refs[n_pairs + i][...], preferred_element_type=F32)
    o_ref[...] = acc


def _matmul_res(acts, weights, resid, tm, tn):
    m, n = resid.shape
    n_pairs = len(acts)
    in_specs = [pl.BlockSpec((tm, a.shape[1]), lambda i, j: (i, 0)) for a in acts]
    in_specs += [pl.BlockSpec((w.shape[0], tn), lambda i, j: (0, j)) for w in weights]
    in_specs += [pl.BlockSpec((tm, tn), lambda i, j: (i, j))]
    return pl.pallas_call(
        functools.partial(_matmul_res_kernel, n_pairs=n_pairs),
        grid=(m // tm, n // tn),
        in_specs=in_specs,
        out_specs=pl.BlockSpec((tm, tn), lambda i, j: (i, j)),
        out_shape=jax.ShapeDtypeStruct((m, n), F32),
        compiler_params=_params(("arbitrary", "arbitrary")),
        name="matmul_residual",
    )(*acts, *weights, resid)


def _ffn_kernel(h_ref, g_ref, wg_ref, wu_ref, wd_ref, gf_ref, o_ref, xn_ref, acc_ref, *, final_norm):
    k = pl.program_id(1)

    @pl.when(k == 0)
    def _():
        xn_ref[...] = _rms_scale(h_ref[...], g_ref[...]).astype(BF16)
        acc_ref[...] = jnp.zeros_like(acc_ref)

    xn = xn_ref[...]
    a = jnp.dot(xn, wg_ref[...], preferred_element_type=F32)
    u = jnp.dot(xn, wu_ref[...], preferred_element_type=F32)
    act = (a * jax.nn.sigmoid(a) * u).astype(BF16)
    acc_ref[...] += jnp.dot(act, wd_ref[...], preferred_element_type=F32)

    @pl.when(k == pl.num_programs(1) - 1)
    def _():
        out = h_ref[...] + acc_ref[...]
        if final_norm:
            out = _rms_scale(out, gf_ref[...])
        o_ref[...] = out


def _ffn(h, gain, w_gate, w_up, w_down, gain_final, final_norm, tm, tk):
    m, d = h.shape
    dff = w_gate.shape[1]
    return pl.pallas_call(
        functools.partial(_ffn_kernel, final_norm=final_norm),
        grid=(m // tm, dff // tk),
        in_specs=[
            pl.BlockSpec((tm, d), lambda i, k: (i, 0)),
            pl.BlockSpec((1, d), lambda i, k: (0, 0)),
            pl.BlockSpec((d, tk), lambda i, k: (0, k)),
            pl.BlockSpec((d, tk), lambda i, k: (0, k)),
            pl.BlockSpec((tk, d), lambda i, k: (k, 0)),
            pl.BlockSpec((1, d), lambda i, k: (0, 0)),
        ],
        out_specs=pl.BlockSpec((tm, d), lambda i, k: (i, 0)),
        out_shape=jax.ShapeDtypeStruct((m, d), F32),
        scratch_shapes=[pltpu.VMEM((tm, d), BF16), pltpu.VMEM((tm, d), F32)],
        compiler_params=_params(("arbitrary", "arbitrary")),
        name="ffn_swiglu",
    )(h, gain.reshape(1, d), w_gate, w_up, w_down, gain_final.reshape(1, d))


TAIL_ROWS = 8


def _conv_in_kernel(x_ref, g_ref, wb_ref, wc_ref, wz_ref, cw_ref, y_ref, xn_ref, tail_ref, *, tiles_per_seq):
    i = pl.program_id(0)
    j = pl.program_id(1)

    @pl.when(j == 0)
    def _():
        xn_ref[...] = _rms_scale(x_ref[...], g_ref[...]).astype(BF16)

    xn = xn_ref[...]
    bg = jnp.dot(xn, wb_ref[...], preferred_element_type=F32)
    z = jnp.dot(xn, wc_ref[...], preferred_element_type=F32) * jnp.dot(xn, wz_ref[...], preferred_element_type=F32)
    tm = z.shape[0]
    seq_start = (i % tiles_per_seq) == 0
    prev = jnp.where(seq_start, 0.0, tail_ref[j])
    tail_ref[j] = z[tm - TAIL_ROWS:, :]
    prev1 = prev[TAIL_ROWS - 1:TAIL_ROWS, :]
    prev2 = prev[TAIL_ROWS - 2:TAIL_ROWS - 1, :]
    row = lax.broadcasted_iota(jnp.int32, (tm, 1), 0)
    z1 = jnp.where(row >= 1, pltpu.roll(z, 1, 0), prev1)
    z2 = jnp.where(row >= 2, pltpu.roll(z, 2, 0), jnp.where(row == 1, prev1, prev2))
    zc = cw_ref[2:3, :] * z + cw_ref[1:2, :] * z1 + cw_ref[0:1, :] * z2
    y_ref[...] = (bg * zc).astype(y_ref.dtype)


def _conv_in(h, gain, w_in, conv_w, seq, tm, tn):
    m, d = h.shape
    cw = conv_w.shape[1]
    nj = cw // tn
    return pl.pallas_call(
        functools.partial(_conv_in_kernel, tiles_per_seq=seq // tm),
        grid=(m // tm, nj),
        in_specs=[
            pl.BlockSpec((tm, d), lambda i, j: (i, 0)),
            pl.BlockSpec((1, d), lambda i, j: (0, 0)),
            pl.BlockSpec((d, tn), lambda i, j: (0, j)),
            pl.BlockSpec((d, tn), lambda i, j: (0, nj + j)),
            pl.BlockSpec((d, tn), lambda i, j: (0, 2 * nj + j)),
            pl.BlockSpec((conv_w.shape[0], tn), lambda i, j: (0, j)),
        ],
        out_specs=pl.BlockSpec((tm, tn), lambda i, j: (i, j)),
        out_shape=jax.ShapeDtypeStruct((m, cw), BF16),
        scratch_shapes=[pltpu.VMEM((tm, d), BF16), pltpu.VMEM((nj, TAIL_ROWS, tn), F32)],
        compiler_params=_params(("arbitrary", "arbitrary")),
        name="conv_mixer_in",
    )(h, gain.reshape(1, d), w_in, w_in, w_in, conv_w)


def _gate_weight(w_in):
    per_group = NSA_REP * N_BRANCH
    cols = []
    for g in range(NSA_KV_GROUPS):
        blk = w_in[:, COL_GT + g * per_group:COL_GT + (g + 1) * per_group]
        cols.append(jnp.pad(blk, ((0, 0), (0, LANES - per_group))))
    return jnp.concatenate(cols, axis=1)


def _pick_tile(m, pref):
    t = min(m, pref)
    while m % t:
        t //= 2
    return t


def kernel(x, norm_mix, norm_ffn, norm_f, w_in_ab, w_out_ab, sgu_w, sgu_b, sgu_g, cmp_pe_k, cmp_w1_k, cmp_w2_k, cmp_pe_v, cmp_w1_v, cmp_w2_v, w_in_c, conv_w, w_out_c, w_gate, w_up, w_down):
    b, s, d = x.shape
    m = b * s
    tm = _pick_tile(m, ROW_TILE)
    tm_wide = _pick_tile(s, 2 * ROW_TILE)
    top_n = min(SLC_TOP_N, s // SLC_LEN)
    slopes = (2.0 ** (-8.0 * jnp.arange(1, NSA_HEADS + 1, dtype=F32) / NSA_HEADS)).astype(F32)
    imp_mat = jnp.asarray(_importance_matrix(s // CMP_STRIDE, s // SLC_LEN), BF16)

    h = x.reshape(m, d)

    w_in = w_in_ab[0]
    proj, gates_raw = _norm_matmul(h, norm_mix[0], w_in[:, :COL_GT].astype(BF16), _gate_weight(w_in).astype(BF16),
                                   tm_wide, COL_GT // 3)
    y_a = _sgu(proj, sgu_w[0], sgu_b[0], sgu_g[0], tm)
    proj3 = proj.reshape(b, s, COL_GT)
    cmp_kv = _compress(
        proj3,
        jnp.stack([cmp_pe_k[0], cmp_pe_v[0]]),
        jnp.stack([cmp_w1_k[0], cmp_w1_v[0]]).astype(BF16),
        jnp.stack([cmp_w2_k[0], cmp_w2_v[0]]).astype(BF16),
    )
    o_cmp, idx = _cmp_select(proj3, cmp_kv, slopes, imp_mat, top_n)
    o_slc = _slc_attention(proj3, idx, slopes, top_n)
    y_b = _win_combine(proj3, o_cmp, o_slc, gates_raw, slopes)
    w_out = w_out_ab[0].astype(BF16)
    h = _matmul_res([y_a, y_b.reshape(m, NSA_WIDTH)], [w_out[:SGU_WIDTH], w_out[SGU_WIDTH:]], h, tm, d)
    h = _ffn(h, norm_ffn[0], w_gate[0].astype(BF16), w_up[0].astype(BF16), w_down[0].astype(BF16), norm_f, False, tm, COL_TILE)

    y_c = _conv_in(h, norm_mix[1], w_in_c[0].astype(BF16), conv_w[0], s, tm_wide, COL_TILE)
    h = _matmul_res([y_c], [w_out_c[0].astype(BF16)], h, tm, d)
    h = _ffn(h, norm_ffn[1], w_gate[1].astype(BF16), w_up[1].astype(BF16), w_down[1].astype(BF16), norm_f, True, tm, COL_TILE)
    return h.reshape(b, s, d)
```
